```python
import jax, jax.numpy as jnp
from jax import lax
import numpy as np

D_MODEL = 1024
BATCH = 8
SEQ = 2048
DEPTH = 1
DEC_BATCH = 128
DEC_SEQ = 4
PAST_LEN = 16384
PAGE_SIZE = 128

HGRN_KD = 128
HGRN_HEADS = D_MODEL // HGRN_KD
HGRN_VD = D_MODEL // HGRN_HEADS
HGRN_K = HGRN_HEADS * HGRN_KD
HGRN_V = HGRN_HEADS * HGRN_VD
CHUNK = 16
CONV_DIM = D_MODEL
CONV_WIDTH = 31
D_FF = -(-8 * D_MODEL // (3 * 256)) * 256
EPS = 1e-6
IN_SIZES = (HGRN_K, HGRN_K, HGRN_V, HGRN_V, CONV_DIM, CONV_DIM, D_MODEL, D_MODEL)
IN_COLS = sum(IN_SIZES)
IN_SPLITS = tuple(int(s) for s in np.cumsum(IN_SIZES)[:-1])

kernel_name = "hgrn2_conformer_gated_hybrid_step"


def rmsnorm(x, w):
    xf = x.astype(jnp.float32)
    y = xf * lax.rsqrt(jnp.mean(xf * xf, axis=-1, keepdims=True) + EPS)
    return (y * w.astype(jnp.float32)).astype(x.dtype)


def chunked_gated_recurrence(q, logf, k, v, s0):
    n, l, h, kd = q.shape
    vd = v.shape[-1]
    c = min(CHUNK, l)
    pad = (-l) % c
    if pad:
        pw = ((0, 0), (0, pad), (0, 0), (0, 0))
        q, logf, k, v = (jnp.pad(a, pw) for a in (q, logf, k, v))
    nc = (l + pad) // c

    def to_chunks(a):
        return a.reshape(n, nc, c, h, a.shape[-1]).transpose(1, 0, 3, 2, 4)

    qs, gs, ks, vs = (to_chunks(a) for a in (q, logf, k, v))
    mask = jnp.tril(jnp.ones((c, c), dtype=bool))[:, :, None]

    def step(s, inp):
        qc, gc, kc, vc = inp
        b = jnp.cumsum(gc, axis=2)
        o_inter = jnp.einsum('nhtk,nhkv->nhtv', qc * jnp.exp(b), s)
        diff = b[:, :, :, None, :] - b[:, :, None, :, :]
        decay = jnp.where(mask, jnp.exp(jnp.where(mask, diff, 0.0)), 0.0)
        att = jnp.einsum('nhtk,nhtsk,nhsk->nhts', qc, decay, kc)
        o_intra = jnp.einsum('nhts,nhsv->nhtv', att, vc)
        b_last = b[:, :, -1:, :]
        s_new = jnp.exp(b_last[:, :, 0, :])[..., None] * s + jnp.einsum(
            'nhsk,nhsv->nhkv', kc * jnp.exp(b_last - b), vc)
        return s_new, o_inter + o_intra

    s_fin, o = lax.scan(step, s0, (qs, gs, ks, vs))
    o = o.transpose(1, 0, 3, 2, 4).reshape(n, nc * c, h, vd)[:, :l]
    return o, s_fin


def hgrn2_branch(q_raw, f_raw, i_raw, og_raw, lb, norm_w, s0):
    n, l, _ = q_raw.shape
    q = jax.nn.silu(q_raw.astype(jnp.float32)).reshape(n, l, HGRN_HEADS, HGRN_KD)
    f = lb + (1.0 - lb) * jax.nn.sigmoid(f_raw.astype(jnp.float32))
    logf = jnp.log(f).reshape(n, l, HGRN_HEADS, HGRN_KD)
    k = (1.0 - f).reshape(n, l, HGRN_HEADS, HGRN_KD)
    v = i_raw.astype(jnp.float32).reshape(n, l, HGRN_HEADS, HGRN_VD)
    o, s_fin = chunked_gated_recurrence(q, logf, k, v, s0.astype(jnp.float32))
    o = o * lax.rsqrt(jnp.mean(o * o, axis=-1, keepdims=True) + EPS)
    o = o.reshape(n, l, HGRN_V) * norm_w.astype(jnp.float32) * jax.nn.silu(og_raw.astype(jnp.float32))
    return o.astype(q_raw.dtype), s_fin


def conformer_conv_branch(a, b, buf, dw_w, dw_b, ln_w, ln_b):
    u = a * jax.nn.sigmoid(b)
    ext = jnp.concatenate([buf.astype(u.dtype), u], axis=1)
    y = lax.conv_general_dilated(ext, dw_w[:, None, :].astype(u.dtype), (1,), 'VALID',
                                 dimension_numbers=('NWC', 'WIO', 'NWC'),
                                 feature_group_count=CONV_DIM) + dw_b
    yf = y.astype(jnp.float32)
    mu = jnp.mean(yf, axis=-1, keepdims=True)
    var = jnp.mean(jnp.square(yf - mu), axis=-1, keepdims=True)
    yf = (yf - mu) * lax.rsqrt(var + EPS) * ln_w.astype(jnp.float32) + ln_b.astype(jnp.float32)
    out = jax.nn.silu(yf).astype(a.dtype)
    return out, ext[:, -(CONV_WIDTH - 1):, :]


def decoder_layer(x, c, s0, buf, w_ada, b_ada, norm_mix_w, w_in, b_in, lb, hgrn_norm_w,
                  conv_dw_w, conv_dw_b, conv_ln_w, conv_ln_b, w_out, norm_ffn_w, w_ffn_in, w_ffn_out):
    mod = jax.nn.silu(c) @ w_ada + b_ada
    sh1, sc1, g1, sh2, sc2, g2 = (m[:, None, :] for m in jnp.split(mod, 6, axis=-1))
    h = rmsnorm(x, norm_mix_w) * (1.0 + sc1) + sh1
    z = h @ w_in + b_in
    q_raw, f_raw, i_raw, og_raw, glu_a, glu_b, gate_a, gate_b = jnp.split(z, IN_SPLITS, axis=-1)
    o_a, s_new = hgrn2_branch(q_raw, f_raw, i_raw, og_raw, lb, hgrn_norm_w, s0)
    o_b, buf_new = conformer_conv_branch(glu_a, glu_b, buf, conv_dw_w, conv_dw_b, conv_ln_w, conv_ln_b)
    merged = jnp.concatenate([jax.nn.sigmoid(gate_a) * o_a, jax.nn.sigmoid(gate_b) * o_b], axis=-1) @ w_out
    x = x + g1 * merged
    h2 = rmsnorm(x, norm_ffn_w) * (1.0 + sc2) + sh2
    gg, uu = jnp.split(h2 @ w_ffn_in, 2, axis=-1)
    x = x + g2 * ((jax.nn.silu(gg) * uu) @ w_ffn_out)
    return x, s_new, buf_new


def setup_inputs(seed: int = 0) -> dict:
    key = jax.random.key(seed)
    ks = jax.random.split(key, 24)
    f32 = jnp.float32
    nrm = lambda k, shape, s: jax.random.normal(k, shape, f32) * s
    d = D_MODEL
    return {
        "x_prompt": nrm(ks[0], (BATCH, SEQ, d), 1.0),
        "x_sample": nrm(ks[1], (DEC_BATCH, DEC_SEQ, d), 1.0),
        "state_hgrn": nrm(ks[2], (DEPTH, DEC_BATCH, HGRN_HEADS, HGRN_KD, HGRN_VD), 0.5),
        "state_conv": nrm(ks[3], (DEPTH, DEC_BATCH, CONV_WIDTH - 1, CONV_DIM), 0.5),
        "c_prompt": nrm(ks[4], (BATCH, d), 1.0),
        "c_sample": nrm(ks[5], (DEC_BATCH, d), 1.0),
        "w_ada": nrm(ks[6], (DEPTH, d, 6 * d), 0.5 * d ** -0.5),
        "b_ada": nrm(ks[7], (DEPTH, 6 * d), 0.02),
        "norm_mix_w": 1.0 + nrm(ks[8], (DEPTH, d), 0.02),
        "w_in": nrm(ks[9], (DEPTH, d, IN_COLS), d ** -0.5),
        "b_in": nrm(ks[10], (DEPTH, IN_COLS), 0.02),
        "lb_logits": nrm(ks[11], (DEPTH + 1, HGRN_K), 1.0),
        "hgrn_norm_w": 1.0 + nrm(ks[12], (DEPTH, HGRN_V), 0.02),
        "conv_dw_w": nrm(ks[13], (DEPTH, CONV_WIDTH, CONV_DIM), CONV_WIDTH ** -0.5),
        "conv_dw_b": nrm(ks[14], (DEPTH, CONV_DIM), 0.02),
        "conv_ln_w": 1.0 + nrm(ks[15], (DEPTH, CONV_DIM), 0.02),
        "conv_ln_b": nrm(ks[16], (DEPTH, CONV_DIM), 0.02),
        "w_out": nrm(ks[17], (DEPTH, HGRN_V + CONV_DIM, d), (HGRN_V + CONV_DIM) ** -0.5),
        "norm_ffn_w": 1.0 + nrm(ks[18], (DEPTH, d), 0.02),
        "w_ffn_in": nrm(ks[19], (DEPTH, d, 2 * D_FF), d ** -0.5),
        "w_ffn_out": nrm(ks[20], (DEPTH, D_FF, d), D_FF ** -0.5),
        "norm_final_w": 1.0 + nrm(ks[21], (d,), 0.02),
    }


def reference(x_prompt, x_sample, state_hgrn, state_conv, c_prompt, c_sample, w_ada, b_ada,
              norm_mix_w, w_in, b_in, lb_logits, hgrn_norm_w, conv_dw_w, conv_dw_b, conv_ln_w,
              conv_ln_b, w_out, norm_ffn_w, w_ffn_in, w_ffn_out, norm_final_w):
    lb_table = jnp.cumsum(jax.nn.softmax(lb_logits.astype(jnp.float32), axis=0), axis=0)
    n_p = x_prompt.shape[0]
    xp, xs = x_prompt, x_sample
    sp_list, bp_list, ss_list, bs_list = [], [], [], []
    for layer in range(DEPTH):
        lw = (w_ada[layer], b_ada[layer], norm_mix_w[layer], w_in[layer], b_in[layer], lb_table[layer],
              hgrn_norm_w[layer], conv_dw_w[layer], conv_dw_b[layer], conv_ln_w[layer], conv_ln_b[layer],
              w_out[layer], norm_ffn_w[layer], w_ffn_in[layer], w_ffn_out[layer])
        s0_p = jnp.zeros((n_p, HGRN_HEADS, HGRN_KD, HGRN_VD), jnp.float32)
        buf0_p = jnp.zeros((n_p, CONV_WIDTH - 1, CONV_DIM), xp.dtype)
        xp, sp, bp = decoder_layer(xp, c_prompt, s0_p, buf0_p, *lw)
        xs, ss, bs = decoder_layer(xs, c_sample, state_hgrn[layer], state_conv[layer], *lw)
        sp_list.append(sp.astype(state_hgrn.dtype))
        bp_list.append(bp.astype(state_conv.dtype))
        ss_list.append(ss.astype(state_hgrn.dtype))
        bs_list.append(bs.astype(state_conv.dtype))
    y_prompt = rmsnorm(xp, norm_final_w)
    y_sample = rmsnorm(xs, norm_final_w)
    new_state_hgrn_prompt = jnp.stack(sp_list, axis=0)
    new_state_conv_prompt = jnp.stack(bp_list, axis=0)
    new_state_hgrn_sample = jnp.stack(ss_list, axis=0)
    new_state_conv_sample = jnp.stack(bs_list, axis=0)
    return (y_prompt, y_sample, new_state_hgrn_prompt, new_state_conv_prompt,
            new_state_hgrn_sample, new_state_conv_sample)
```

```python
import functools

import jax
import jax.numpy as jnp
from jax import lax
from jax.experimental import pallas as pl
from jax.experimental.pallas import tpu as pltpu

F32 = jnp.float32
BF16 = jnp.bfloat16

EPS = 1e-6
HEAD_DIM = 128
CONV_WIDTH = 31
LANES = 128
SUBLANES = 8
VMEM_LIMIT_BYTES = 56 * 1024 * 1024

IN_SEGMENTS = 8
MOD_ROWS = 6


def _silu(x):
    return x * jax.nn.sigmoid(x)


def _params(*sem):
    return pltpu.CompilerParams(dimension_semantics=sem, vmem_limit_bytes=VMEM_LIMIT_BYTES)


def _const_spec(shape):
    nd = len(shape)
    return pl.BlockSpec(shape, lambda *_: (0,) * nd, pipeline_mode=pl.Buffered(1))


def _mod_kernel(c_ref, w_ref, b_ref, o_ref):
    a = _silu(c_ref[...]).astype(BF16)
    o_ref[...] = jnp.dot(a, w_ref[...].astype(BF16), preferred_element_type=F32) + b_ref[...]


def _mod(c, w_ada, b_ada, *, tn=512):
    n, d = c.shape
    cols = w_ada.shape[1]
    return pl.pallas_call(
        _mod_kernel,
        grid=(cols // tn,),
        in_specs=[pl.BlockSpec((n, d), lambda j: (0, 0)),
                  pl.BlockSpec((d, tn), lambda j: (0, j)),
                  pl.BlockSpec((1, tn), lambda j: (0, j))],
        out_specs=pl.BlockSpec((n, tn), lambda j: (0, j)),
        out_shape=jax.ShapeDtypeStruct((n, cols), F32),
        compiler_params=_params("parallel"),
        name="mod",
    )(c, w_ada, b_ada.reshape(1, cols))


def _inproj_kernel(x_ref, sh_ref, sc_ref, nw_ref, w_ref, b_ref, lbl_ref,
                   q_ref, f_ref, v_ref, ga_ref, u_ref, sgb_ref, h_scr, *, layer, tc):
    x = x_ref[...]
    ms = jnp.mean(x * x, axis=-1, keepdims=True)
    h = x * lax.rsqrt(ms + EPS) * nw_ref[...]
    h = h * (1.0 + sc_ref[...]) + sh_ref[...]
    h_scr[...] = h.astype(BF16)
    nc = w_ref.shape[0]
    for j in range(nc):
        z = jnp.dot(h_scr[...], w_ref[j], preferred_element_type=F32) + b_ref[j]
        seg = [z[:, s * tc:(s + 1) * tc] for s in range(IN_SEGMENTS)]
        q_raw, f_raw, i_raw, og_raw, glu_a, glu_b, gate_a, gate_b = seg
        lg = lbl_ref[j]
        e = jnp.exp(lg - jnp.max(lg, axis=0, keepdims=True))
        lb = jnp.sum(e[:layer + 1], axis=0, keepdims=True) / jnp.sum(e, axis=0, keepdims=True)
        col = slice(j * tc, (j + 1) * tc)
        q_ref[:, col] = _silu(q_raw)
        f_ref[:, col] = lb + (1.0 - lb) * jax.nn.sigmoid(f_raw)
        v_ref[:, col] = i_raw
        ga_ref[:, col] = _silu(og_raw) * jax.nn.sigmoid(gate_a)
        u_ref[:, col] = glu_a * jax.nn.sigmoid(glu_b)
        sgb_ref[:, col] = jax.nn.sigmoid(gate_b)


def _inproj(x, mod, nw, w_r, b_r, lbl_r, *, layer, tm, rows_per_mod):
    m, d = x.shape
    tc = w_r.shape[2] // IN_SEGMENTS
    r = mod.shape[2]
    tiles_per_mod = rows_per_mod // tm
    mod_spec = lambda row: pl.BlockSpec((None, None, r, d), lambda i: (i // tiles_per_mod, row, 0, 0))
    out_spec = pl.BlockSpec((tm, d), lambda i: (i, 0))
    out_shape = jax.ShapeDtypeStruct((m, d), F32)
    return pl.pallas_call(
        functools.partial(_inproj_kernel, layer=layer, tc=tc),
        grid=(m // tm,),
        in_specs=[pl.BlockSpec((tm, d), lambda i: (i, 0)), mod_spec(0), mod_spec(1),
                  _const_spec((1, d)), _const_spec(w_r.shape), _const_spec(b_r.shape),
                  _const_spec(lbl_r.shape)],
        out_specs=[out_spec] * 6,
        out_shape=[out_shape] * 6,
        scratch_shapes=[pltpu.VMEM((tm, d), BF16)],
        compiler_params=_params("parallel"),
        name="inproj",
    )(x, mod, mod, nw, w_r, b_r, lbl_r)


def _tree_products(f3):
    c = f3.shape[0]
    tail = f3.shape[1:]
    p = f3
    r = jnp.ones_like(f3)
    ps, rs = [p], [r]
    half = 1
    while half < c:
        blk = 2 * half
        p4 = p.reshape((c // blk, blk) + tail)
        r4 = r.reshape((c // blk, blk) + tail)
        left_tot = p4[:, half - 1:half]
        right_tot = p4[:, blk - 1:blk]
        p = jnp.concatenate([p4[:, :half], p4[:, half:] * left_tot], axis=1).reshape((c,) + tail)
        r = jnp.concatenate([r4[:, :half] * right_tot, r4[:, half:]], axis=1).reshape((c,) + tail)
        ps.append(p)
        rs.append(r)
        half = blk
    return ps, rs


def _level_masks(c):
    t = lax.broadcasted_iota(jnp.int32, (c, c), 0)
    s = lax.broadcasted_iota(jnp.int32, (c, c), 1)
    masks = []
    half = 1
    while half < c:
        blk = 2 * half
        same = (t & -blk) == (s & -blk)
        masks.append(same & ((t & half) != 0) & ((s & half) == 0))
        half = blk
    return masks


def _recur_kernel(*refs, heads, chunk, has_init):
    if has_init:
        q_ref, f_ref, v_ref, ga_ref, nw_ref, s0_ref, o_ref, s_out_ref, st_scr, p_scr, r_scr = refs
    else:
        q_ref, f_ref, v_ref, ga_ref, nw_ref, o_ref, s_out_ref, st_scr, p_scr, r_scr = refs
    c_idx = pl.program_id(1)
    nlev = p_scr.shape[0]

    @pl.when(c_idx == 0)
    def _():
        if has_init:
            for h in range(heads):
                st_scr[h] = s0_ref[h].T
        else:
            st_scr[...] = jnp.zeros_like(st_scr)

    f3 = f_ref[...].reshape(chunk, heads, HEAD_DIM)
    ps, rs = _tree_products(f3)
    for l in range(nlev):
        p_scr[l] = ps[l].reshape(chunk * heads, HEAD_DIM)
        r_scr[l] = rs[l].reshape(chunk * heads, HEAD_DIM)

    masks = _level_masks(chunk)
    nt = (((1,), (1,)), ((), ()))
    tn = (((0,), (0,)), ((), ()))

    def head_body(h, carry):
        rows = pl.ds(h, chunk, stride=heads)
        q = q_ref[rows, :]
        f = f_ref[rows, :]
        v = v_ref[rows, :]
        k = 1.0 - f
        vb = v.astype(BF16)
        att = jnp.zeros((chunk, chunk), F32)
        for l in range(1, nlev):
            ql = (q * p_scr[l - 1, rows, :]).astype(BF16)
            kl = (k * r_scr[l - 1, rows, :]).astype(BF16)
            a = lax.dot_general(ql, kl, nt, preferred_element_type=F32)
            att = att + jnp.where(masks[l - 1], a, 0.0)
        p_full = p_scr[nlev - 1, rows, :]
        r_full = r_scr[nlev - 1, rows, :]
        st0 = st_scr[h]
        qt = (q * p_full).astype(BF16)
        o = lax.dot_general(qt, st0.astype(BF16), nt, preferred_element_type=F32)
        o = o + jnp.dot(att.astype(BF16), vb, preferred_element_type=F32)
        o = o + jnp.sum(q * k, axis=-1, keepdims=True) * v
        kt = (k * r_full).astype(BF16)
        st_scr[h] = st0 * p_full[chunk - 1:chunk, :] + lax.dot_general(
            vb, kt, tn, preferred_element_type=F32)
        o = o * lax.rsqrt(jnp.mean(o * o, axis=-1, keepdims=True) + EPS)
        o_ref[rows, :] = o * nw_ref[pl.ds(h, 1), :] * ga_ref[rows, :]
        return carry

    lax.fori_loop(0, heads, head_body, 0)

    @pl.when(c_idx == pl.num_programs(1) - 1)
    def _():
        for h in range(heads):
            s_out_ref[h] = st_scr[h].T


def _recur(q, f, v, ga, norm_w, s0, *, heads, chunk):
    n, l, hd = q.shape
    nlev = chunk.bit_length()
    rows = chunk * heads
    to3 = lambda a: a.reshape(n, l * heads, HEAD_DIM)
    blk = pl.BlockSpec((None, rows, HEAD_DIM), lambda i, c: (i, c, 0))
    st_spec = pl.BlockSpec((None, heads, HEAD_DIM, HEAD_DIM), lambda i, c: (i, 0, 0, 0))
    has_init = s0 is not None
    o, s = pl.pallas_call(
        functools.partial(_recur_kernel, heads=heads, chunk=chunk, has_init=has_init),
        grid=(n, l // chunk),
        in_specs=[blk, blk, blk, blk, _const_spec((heads, HEAD_DIM))] + ([st_spec] if has_init else []),
        out_specs=[blk, st_spec],
        out_shape=[jax.ShapeDtypeStruct((n, l * heads, HEAD_DIM), F32),
                   jax.ShapeDtypeStruct((n, heads, HEAD_DIM, HEAD_DIM), F32)],
        scratch_shapes=[pltpu.VMEM((heads, HEAD_DIM, HEAD_DIM), F32),
                        pltpu.VMEM((nlev, rows, HEAD_DIM), F32),
                        pltpu.VMEM((nlev, rows, HEAD_DIM), F32)],
        compiler_params=_params("parallel", "arbitrary"),
        name="recur",
    )(to3(q), to3(f), to3(v), to3(ga), norm_w.reshape(heads, HEAD_DIM), *([s0] if has_init else []))
    return o.reshape(n, l, hd), s


def _conv_kernel(*refs, tile, tb, has_init):
    if has_init:
        u_ref, sgb_ref, w_ref, b_ref, lnw_ref, lnb_ref, buf_ref, o_ref, buf_out_ref, ext_scr = refs
    else:
        u_ref, sgb_ref, w_ref, b_ref, lnw_ref, lnb_ref, o_ref, buf_out_ref, ext_scr = refs
    t_idx = pl.program_id(1)
    hist = CONV_WIDTH - 1

    @pl.when(t_idx == 0)
    def _():
        if has_init:
            ext_scr[0:hist] = buf_ref[...]
        else:
            ext_scr[0:hist] = jnp.zeros((hist,) + ext_scr.shape[1:], F32)

    @pl.when(t_idx > 0)
    def _():
        ext_scr[0:hist] = ext_scr[tile:tile + hist]

    ext_scr[hist:hist + tile] = u_ref[...]

    n_ch = ext_scr.shape[1] * ext_scr.shape[2]

    def block_body(i, carry):
        r0 = pl.multiple_of(i * tb, tb)
        acc = jnp.broadcast_to(b_ref[...], (tb,) + b_ref.shape[1:])
        for j in range(CONV_WIDTH):
            acc = acc + w_ref[j] * ext_scr[pl.ds(r0 + j, tb)]
        s1 = jnp.sum(jnp.sum(acc, axis=2, keepdims=True), axis=1, keepdims=True)
        mu = s1 * (1.0 / n_ch)
        d = acc - mu
        s2 = jnp.sum(jnp.sum(d * d, axis=2, keepdims=True), axis=1, keepdims=True)
        y = d * lax.rsqrt(s2 * (1.0 / n_ch) + EPS) * lnw_ref[...] + lnb_ref[...]
        o_ref[pl.ds(r0, tb)] = _silu(y) * sgb_ref[pl.ds(r0, tb)]
        return carry

    lax.fori_loop(0, tile // tb, block_body, 0)

    @pl.when(t_idx == pl.num_programs(1) - 1)
    def _():
        buf_out_ref[...] = ext_scr[tile:tile + hist]


def _conv(u, sgb, dw_w, dw_b, ln_w, ln_b, buf, *, tile, tb):
    n, l, ch = u.shape
    cs = ch // LANES
    hist = CONV_WIDTH - 1
    to4 = lambda a: a.reshape(a.shape[:-1] + (cs, LANES))
    blk = pl.BlockSpec((None, tile, cs, LANES), lambda i, t: (i, t, 0, 0))
    buf_spec = pl.BlockSpec((None, hist, cs, LANES), lambda i, t: (i, 0, 0, 0))
    has_init = buf is not None
    o, buf_new = pl.pallas_call(
        functools.partial(_conv_kernel, tile=tile, tb=tb, has_init=has_init),
        grid=(n, l // tile),
        in_specs=[blk, blk, _const_spec((CONV_WIDTH, cs, LANES)), _const_spec((1, cs, LANES)),
                  _const_spec((1, cs, LANES)), _const_spec((1, cs, LANES))]
                 + ([buf_spec] if has_init else []),
        out_specs=[blk, buf_spec],
        out_shape=[jax.ShapeDtypeStruct((n, l, cs, LANES), F32),
                   jax.ShapeDtypeStruct((n, hist, cs, LANES), F32)],
        scratch_shapes=[pltpu.VMEM((tile + hist, cs, LANES), F32)],
        compiler_params=_params("parallel", "arbitrary"),
        name="conv",
    )(to4(u), to4(sgb), to4(dw_w), to4(dw_b.reshape(1, ch)), to4(ln_w.reshape(1, ch)),
      to4(ln_b.reshape(1, ch)), *([to4(buf)] if has_init else []))
    return o.reshape(n, l, ch), buf_new.reshape(n, hist, ch)


def _post_kernel(x_ref, oa_ref, ob_ref, g1_ref, sh2_ref, sc2_ref, g2_ref, wo_ref, nfw_ref,
                 w1_ref, w2_ref, nlw_ref, y_ref, h_scr, acc_scr):
    d = x_ref.shape[1]
    fc = w2_ref.shape[1]
    merged = jnp.dot(oa_ref[...].astype(BF16), wo_ref[0:d, :], preferred_element_type=F32)
    merged = merged + jnp.dot(ob_ref[...].astype(BF16), wo_ref[d:2 * d, :], preferred_element_type=F32)
    x1 = x_ref[...] + g1_ref[...] * merged
    ms = jnp.mean(x1 * x1, axis=-1, keepdims=True)
    h2 = x1 * lax.rsqrt(ms + EPS) * nfw_ref[...]
    h2 = h2 * (1.0 + sc2_ref[...]) + sh2_ref[...]
    h_scr[...] = h2.astype(BF16)
    acc_scr[...] = jnp.zeros_like(acc_scr)

    def ffn_body(c, carry):
        gu = jnp.dot(h_scr[...], w1_ref[c], preferred_element_type=F32)
        a = _silu(gu[:, :fc]) * gu[:, fc:]
        acc_scr[...] += jnp.dot(a.astype(BF16), w2_ref[c], preferred_element_type=F32)
        return carry

    lax.fori_loop(0, w1_ref.shape[0], ffn_body, 0)
    x2 = x1 + g2_ref[...] * acc_scr[...]
    ms2 = jnp.mean(x2 * x2, axis=-1, keepdims=True)
    y_ref[...] = x2 * lax.rsqrt(ms2 + EPS) * nlw_ref[...]


def _post(x, oa, ob, mod, wo, nfw, w1_r, w2_r, nlw, *, tm, rows_per_mod):
    m, d = x.shape
    r = mod.shape[2]
    tiles_per_mod = rows_per_mod // tm
    mod_spec = lambda row: pl.BlockSpec((None, None, r, d), lambda i: (i // tiles_per_mod, row, 0, 0))
    row_spec = pl.BlockSpec((tm, d), lambda i: (i, 0))
    return pl.pallas_call(
        _post_kernel,
        grid=(m // tm,),
        in_specs=[row_spec, row_spec, row_spec, mod_spec(2), mod_spec(3), mod_spec(4), mod_spec(5),
                  _const_spec(wo.shape), _const_spec((1, d)), _const_spec(w1_r.shape),
                  _const_spec(w2_r.shape), _const_spec((1, d))],
        out_specs=row_spec,
        out_shape=jax.ShapeDtypeStruct((m, d), F32),
        scratch_shapes=[pltpu.VMEM((tm, d), BF16), pltpu.VMEM((tm, d), F32)],
        compiler_params=_params("parallel"),
        name="post",
    )(x, oa, ob, mod, mod, mod, mod, wo, nfw, w1_r, w2_r, nlw)


INPROJ_TC = 256
FFN_CHUNK = 256
PROMPT_TM = 512
PROMPT_CHUNK = 128
SAMPLE_CHUNK = 16
PROMPT_CONV_TILE = 256
CONV_TB = 8


def kernel(x_prompt, x_sample, state_hgrn, state_conv, c_prompt, c_sample, w_ada, b_ada, norm_mix_w, w_in, b_in, lb_logits, hgrn_norm_w, conv_dw_w, conv_dw_b, conv_ln_w, conv_ln_b, w_out, norm_ffn_w, w_ffn_in, w_ffn_out, norm_final_w):
    depth = w_in.shape[0]
    assert depth == 1, "single-layer configuration"
    layer = 0
    n_p, l_p, d = x_prompt.shape
    n_s, l_s, _ = x_sample.shape
    heads = d // HEAD_DIM
    d_ff = w_ffn_out.shape[1]

    nc = d // INPROJ_TC
    w_in_r = (w_in[layer].astype(BF16).reshape(d, IN_SEGMENTS, nc, INPROJ_TC)
              .transpose(2, 0, 1, 3).reshape(nc, d, IN_SEGMENTS * INPROJ_TC))
    b_in_r = (b_in[layer].reshape(IN_SEGMENTS, nc, INPROJ_TC).transpose(1, 0, 2)
              .reshape(nc, 1, IN_SEGMENTS * INPROJ_TC))
    lbl_r = lb_logits.reshape(depth + 1, nc, INPROJ_TC).transpose(1, 0, 2)
    nf = d_ff // FFN_CHUNK
    w1 = w_ffn_in[layer].astype(BF16)
    w1_r = jnp.concatenate([w1[:, :d_ff].reshape(d, nf, FFN_CHUNK), w1[:, d_ff:].reshape(d, nf, FFN_CHUNK)],
                           axis=2).transpose(1, 0, 2)
    w2_r = w_ffn_out[layer].astype(BF16).reshape(nf, FFN_CHUNK, d)
    wo = w_out[layer].astype(BF16)
    row = lambda a: a.reshape(1, d)

    mod = _mod(jnp.concatenate([c_prompt, c_sample], axis=0), w_ada[layer], b_ada[layer])
    mod = mod.reshape(n_p + n_s, MOD_ROWS, d)
    mod_p = mod[:n_p].reshape(n_p, MOD_ROWS, 1, d)
    mod_s = jnp.broadcast_to(mod[n_p:, None, :, :], (n_s, l_s, MOD_ROWS, d))
    mod_s = mod_s.transpose(2, 0, 1, 3).reshape(1, MOD_ROWS, n_s * l_s, d)

    def group(x, mod_g, tm, rows_per_mod, s0, buf, chunk, conv_tile, conv_tb):
        n, l, _ = x.shape
        x2 = x.reshape(n * l, d)
        q, f, v, ga, u, sgb = _inproj(x2, mod_g, row(norm_mix_w[layer]), w_in_r, b_in_r, lbl_r,
                                      layer=layer, tm=tm, rows_per_mod=rows_per_mod)
        seq = lambda a: a.reshape(n, l, d)
        q, f, v, ga = seq(q), seq(f), seq(v), seq(ga)
        pad = (-l) % chunk
        if pad:
            pw = ((0, 0), (0, pad), (0, 0))
            q, v, ga = (jnp.pad(a, pw) for a in (q, v, ga))
            f = jnp.pad(f, pw, constant_values=1.0)
        o_a, s_new = _recur(q, f, v, ga, hgrn_norm_w[layer], s0, heads=heads, chunk=chunk)
        o_a = o_a[:, :l]
        o_b, buf_new = _conv(seq(u), seq(sgb), conv_dw_w[layer], conv_dw_b[layer], conv_ln_w[layer],
                             conv_ln_b[layer], buf, tile=conv_tile, tb=conv_tb)
        y = _post(x2, o_a.reshape(n * l, d), o_b.reshape(n * l, d), mod_g, wo, row(norm_ffn_w[layer]),
                  w1_r, w2_r, row(norm_final_w), tm=tm, rows_per_mod=rows_per_mod)
        return y.reshape(n, l, d), s_new, buf_new

    y_p, s_p, b_p = group(x_prompt, mod_p, PROMPT_TM, l_p, None, None, PROMPT_CHUNK, PROMPT_CONV_TILE, CONV_TB)
    y_s, s_s, b_s = group(x_sample, mod_s, n_s * l_s, n_s * l_s, state_hgrn[layer], state_conv[layer],
                          SAMPLE_CHUNK, l_s, l_s)
    return (y_p, y_s, s_p[None], b_p[None], s_s[None], b_s[None])
```

```python
import functools

import numpy as np
import jax
import jax.numpy as jnp
from jax import lax
from jax.experimental import pallas as pl
from jax.experimental.pallas import tpu as pltpu

F32 = jnp.float32
BF16 = jnp.bfloat16

EPS = 1e-6
LANES = 128
SUBLANES = 8
HEAD_DIM = LANES
CONV_WIDTH = 31
CONV_HIST = CONV_WIDTH - 1
VMEM_LIMIT_BYTES = 56 * 1024 * 1024

IN_SEGMENTS = 8
MOD_ROWS = 6

NT = (((1,), (1,)), ((), ()))
TN = (((0,), (0,)), ((), ()))


def _silu(x):
    return x * jax.nn.sigmoid(x)


def _params(*sem):
    return pltpu.CompilerParams(dimension_semantics=sem, vmem_limit_bytes=VMEM_LIMIT_BYTES)


def _const_spec(shape):
    nd = len(shape)
    return pl.BlockSpec(shape, lambda *_: (0,) * nd, pipeline_mode=pl.Buffered(1))


def _group_rows(g, n, groups):
    return pl.ds(g, n, stride=groups)


def _mod_kernel(c_ref, w_ref, b_ref, o_ref):
    a = _silu(c_ref[...]).astype(BF16)
    o_ref[...] = jnp.dot(a, w_ref[...].astype(BF16), preferred_element_type=F32) + b_ref[...]


def _mod(c, w_ada, b_ada, *, tn=512):
    n, d = c.shape
    cols = w_ada.shape[1]
    return pl.pallas_call(
        _mod_kernel,
        grid=(cols // tn,),
        in_specs=[pl.BlockSpec((n, d), lambda j: (0, 0)),
                  pl.BlockSpec((d, tn), lambda j: (0, j)),
                  pl.BlockSpec((1, tn), lambda j: (0, j))],
        out_specs=pl.BlockSpec((n, tn), lambda j: (0, j)),
        out_shape=jax.ShapeDtypeStruct((n, cols), F32),
        compiler_params=_params("parallel"),
        name="mod",
    )(c, w_ada, b_ada.reshape(1, cols))


def _inproj_kernel(x_ref, sh_ref, sc_ref, nw_ref, w_ref, b_ref, lbl_ref,
                   q_ref, f_ref, v_ref, ga_ref, u_ref, sgb_ref, h_scr, *, layer, tc):
    tm, d = x_ref.shape
    groups = d // LANES
    x = x_ref[...]
    ms = jnp.mean(x * x, axis=-1, keepdims=True)
    h = x * lax.rsqrt(ms + EPS) * nw_ref[...]
    h = h * (1.0 + sc_ref[...]) + sh_ref[...]
    h_scr[...] = h.astype(BF16)
    nc = w_ref.shape[0]
    for j in range(nc):
        z = jnp.dot(h_scr[...], w_ref[j], preferred_element_type=F32) + b_ref[j]
        seg = [z[:, s * tc:(s + 1) * tc] for s in range(IN_SEGMENTS)]
        q_raw, f_raw, i_raw, og_raw, glu_a, glu_b, gate_a, gate_b = seg
        lg = lbl_ref[j]
        e = jnp.exp(lg - jnp.max(lg, axis=0, keepdims=True))
        lb = jnp.sum(e[:layer + 1], axis=0, keepdims=True) / jnp.sum(e, axis=0, keepdims=True)
        outs = ((q_ref, _silu(q_raw)),
                (f_ref, lb + (1.0 - lb) * jax.nn.sigmoid(f_raw)),
                (v_ref, i_raw),
                (ga_ref, _silu(og_raw) * jax.nn.sigmoid(gate_a)),
                (u_ref, glu_a * jax.nn.sigmoid(glu_b)),
                (sgb_ref, jax.nn.sigmoid(gate_b)))
        for ref, val in outs:
            for g in range(tc // LANES):
                rows = _group_rows(j * (tc // LANES) + g, tm, groups)
                ref[rows, :] = val[:, g * LANES:(g + 1) * LANES]


def _inproj(x, mod, nw, w_r, b_r, lbl_r, *, layer, tm, rows_per_mod):
    m, d = x.shape
    groups = d // LANES
    tc = w_r.shape[2] // IN_SEGMENTS
    r = mod.shape[2]
    tiles_per_mod = rows_per_mod // tm
    mod_spec = lambda row: pl.BlockSpec((None, None, r, d), lambda i: (i // tiles_per_mod, row, 0, 0))
    out_spec = pl.BlockSpec((tm * groups, LANES), lambda i: (i, 0))
    out_shape = jax.ShapeDtypeStruct((m * groups, LANES), F32)
    return pl.pallas_call(
        functools.partial(_inproj_kernel, layer=layer, tc=tc),
        grid=(m // tm,),
        in_specs=[pl.BlockSpec((tm, d), lambda i: (i, 0)), mod_spec(0), mod_spec(1),
                  _const_spec((1, d)), _const_spec(w_r.shape), _const_spec(b_r.shape),
                  _const_spec(lbl_r.shape)],
        out_specs=[out_spec] * 6,
        out_shape=[out_shape] * 6,
        scratch_shapes=[pltpu.VMEM((tm, d), BF16)],
        compiler_params=_params("parallel"),
        name="inproj",
    )(x, mod, mod, nw, w_r, b_r, lbl_r)


def _split_level_table(c):
    t = np.arange(c)[:, None]
    s = np.arange(c)[None, :]
    x = t ^ s
    lvl = np.zeros((c, c), np.int32)
    for l in range(1, c.bit_length()):
        lvl[(x >> (l - 1)) == 1] = l
    return np.where(t > s, lvl, 0).astype(np.int32)


def _recur_kernel(q_ref, f_ref, v_ref, ga_ref, nw_ref, lvl_ref, o_ref, s_out_ref,
                  st_scr, x_scr, d_scr, *, heads, chunk):
    c_idx = pl.program_id(1)
    nlev = chunk.bit_length() - 1

    @pl.when(c_idx == 0)
    def _():
        st_scr[...] = jnp.zeros_like(st_scr)

    def to3(a):
        return a.reshape(chunk, heads, HEAD_DIM)

    def flat(a):
        return a.reshape(chunk * heads, HEAD_DIM)

    f3 = to3(f_ref[...])
    q3 = to3(q_ref[...])
    v3 = to3(v_ref[...])
    k3 = 1.0 - f3
    p, r = f3, jnp.ones_like(f3)
    half = 1
    for l in range(nlev):
        blk = 2 * half
        shp = (chunk // blk, blk, heads, HEAD_DIM)
        p4, r4, q4, k4 = (a.reshape(shp) for a in (p, r, q3, k3))
        x = jnp.concatenate([k4[:, :half] * r4[:, :half], q4[:, half:] * p4[:, half:]], axis=1)
        x_scr[l] = flat(x)
        left_tot = p4[:, half - 1:half]
        right_tot = p4[:, blk - 1:blk]
        p = jnp.concatenate([p4[:, :half], p4[:, half:] * left_tot], axis=1).reshape(f3.shape)
        r = jnp.concatenate([r4[:, :half] * right_tot, r4[:, half:]], axis=1).reshape(f3.shape)
        half = blk
    x_scr[nlev] = flat(q3 * p)
    x_scr[nlev + 1] = flat(k3 * r)
    d_scr[...] = flat(jnp.sum(q3 * k3, axis=-1, keepdims=True) * v3)
    p_last = p[chunk - 1]

    lvl = lvl_ref[...]
    for h in range(heads):
        rows = _group_rows(h, chunk, heads)
        att = jnp.zeros((chunk, chunk), F32)
        for l in range(nlev):
            xb = x_scr[l, rows, :].astype(BF16)
            a = lax.dot_general(xb, xb, NT, preferred_element_type=F32)
            att = jnp.where(lvl == l + 1, a, att)
        qt = x_scr[nlev, rows, :].astype(BF16)
        kt = x_scr[nlev + 1, rows, :].astype(BF16)
        vb = v_ref[rows, :].astype(BF16)
        st0 = st_scr[h]
        o = lax.dot_general(qt, st0.astype(BF16), NT, preferred_element_type=F32)
        o = o + jnp.dot(att.astype(BF16), vb, preferred_element_type=F32)
        o_ref[rows, :] = o + d_scr[rows, :]
        st_scr[h] = st0 * p_last[h:h + 1, :] + lax.dot_general(vb, kt, TN, preferred_element_type=F32)

    o3 = to3(o_ref[...])
    o3 = o3 * lax.rsqrt(jnp.mean(o3 * o3, axis=-1, keepdims=True) + EPS)
    o_ref[...] = flat(o3 * nw_ref[...] * to3(ga_ref[...]))

    @pl.when(c_idx == pl.num_programs(1) - 1)
    def _():
        for h in range(heads):
            s_out_ref[h] = st_scr[h].T


def _recur(q, f, v, ga, norm_w, *, n, heads, chunk):
    rows_total = q.shape[0]
    l = rows_total // (n * heads)
    nlev = chunk.bit_length() - 1
    rows = chunk * heads
    steps = l // chunk
    blk = pl.BlockSpec((rows, HEAD_DIM), lambda i, c: (i * steps + c, 0))
    st_spec = pl.BlockSpec((None, heads, HEAD_DIM, HEAD_DIM), lambda i, c: (i, 0, 0, 0))
    lvl = jnp.asarray(_split_level_table(chunk))
    return pl.pallas_call(
        functools.partial(_recur_kernel, heads=heads, chunk=chunk),
        grid=(n, steps),
        in_specs=[blk, blk, blk, blk, _const_spec((heads, HEAD_DIM)), _const_spec((chunk, chunk))],
        out_specs=[blk, st_spec],
        out_shape=[jax.ShapeDtypeStruct((rows_total, HEAD_DIM), F32),
                   jax.ShapeDtypeStruct((n, heads, HEAD_DIM, HEAD_DIM), F32)],
        scratch_shapes=[pltpu.VMEM((heads, HEAD_DIM, HEAD_DIM), F32),
                        pltpu.VMEM((nlev + 2, rows, HEAD_DIM), F32),
                        pltpu.VMEM((rows, HEAD_DIM), F32)],
        compiler_params=_params("parallel", "arbitrary"),
        name="recur",
    )(q, f, v, ga, norm_w.reshape(heads, HEAD_DIM), lvl)


def _recur_step_kernel(q_ref, f_ref, v_ref, ga_ref, nw_ref, s0_ref, o_ref, s_out_ref,
                       qt_scr, kt_scr, vp_scr, oi_scr, pt_scr, *, heads, steps, seqs, pad):
    rows_seq = steps * heads
    zero = jnp.zeros((pad * heads, HEAD_DIM), F32)
    qt_scr[...] = zero
    kt_scr[...] = zero
    vp_scr[...] = zero
    pt_scr[...] = jnp.zeros_like(pt_scr)

    def seq_body(g, carry):
        base = pl.multiple_of(g * rows_seq, rows_seq)
        blk = pl.ds(base, rows_seq)
        to3 = lambda a: a.reshape(steps, heads, HEAD_DIM)
        q3, f3, v3 = to3(q_ref[blk, :]), to3(f_ref[blk, :]), to3(v_ref[blk, :])
        k3 = 1.0 - f3
        p = [f3[0]]
        for t in range(1, steps):
            p.append(p[-1] * f3[t])
        r = [None] * steps
        r[steps - 1] = jnp.ones_like(f3[0])
        for s in range(steps - 2, -1, -1):
            r[s] = r[s + 1] * f3[s + 1]
        intra = []
        for t in range(steps):
            acc = jnp.sum(q3[t] * k3[t], axis=-1, keepdims=True) * v3[t]
            dec = None
            for s in range(t - 1, -1, -1):
                dec = f3[s + 1] if dec is None else dec * f3[s + 1]
                acc = acc + jnp.sum(q3[t] * k3[s] * dec, axis=-1, keepdims=True) * v3[s]
            intra.append(acc)
        for t in range(steps):
            rows_t = pl.ds(t * heads, heads)
            qt_scr[rows_t, :] = q3[t] * p[t]
            kt_scr[rows_t, :] = k3[t] * r[t]
            vp_scr[rows_t, :] = v3[t]
        pt_scr[0:heads, :] = p[steps - 1]
        p_col = pt_scr[...].T
        for h in range(heads):
            rows = _group_rows(h, pad, heads)
            qt = qt_scr[rows, :].astype(BF16)
            kt = kt_scr[rows, :].astype(BF16)
            vb = vp_scr[rows, :].astype(BF16)
            s0 = s0_ref[g, h]
            oi_scr[rows, :] = jnp.dot(qt, s0.astype(BF16), preferred_element_type=F32)
            s_out_ref[g, h] = s0 * p_col[:, h:h + 1] + lax.dot_general(
                kt, vb, TN, preferred_element_type=F32)
        o3 = to3(oi_scr[0:rows_seq, :]) + jnp.stack(intra, axis=0)
        o3 = o3 * lax.rsqrt(jnp.mean(o3 * o3, axis=-1, keepdims=True) + EPS)
        o_ref[blk, :] = (o3 * nw_ref[...] * to3(ga_ref[blk, :])).reshape(rows_seq, HEAD_DIM)
        return carry

    lax.fori_loop(0, seqs, seq_body, 0)


def _recur_step(q, f, v, ga, norm_w, s0, *, heads, steps, seqs, pad):
    n = s0.shape[0]
    rows = seqs * steps * heads
    blk = pl.BlockSpec((rows, HEAD_DIM), lambda i: (i, 0))
    st_spec = pl.BlockSpec((seqs, heads, HEAD_DIM, HEAD_DIM), lambda i: (i, 0, 0, 0))
    scr = pltpu.VMEM((pad * heads, HEAD_DIM), F32)
    return pl.pallas_call(
        functools.partial(_recur_step_kernel, heads=heads, steps=steps, seqs=seqs, pad=pad),
        grid=(n // seqs,),
        in_specs=[blk, blk, blk, blk, _const_spec((heads, HEAD_DIM)), st_spec],
        out_specs=[blk, st_spec],
        out_shape=[jax.ShapeDtypeStruct(q.shape, F32), jax.ShapeDtypeStruct(s0.shape, F32)],
        scratch_shapes=[scr, scr, scr, scr, pltpu.VMEM((HEAD_DIM, HEAD_DIM), F32)],
        compiler_params=_params("parallel"),
        name="recur_step",
    )(q, f, v, ga, norm_w.reshape(heads, HEAD_DIM), s0)


def _conv_taps(ext_scr, w_ref, b_ref, y_ref, *, ext_row0, out_row0, tb, groups):
    acc = [jnp.broadcast_to(b_ref[...], (tb, groups, LANES)), jnp.zeros((tb, groups, LANES), F32)]
    for j in range(CONV_WIDTH):
        start = (ext_row0 + j) * groups
        if not isinstance(start, int):
            start = pl.multiple_of(start, groups)
        e = ext_scr[pl.ds(start, tb * groups), :].reshape(tb, groups, LANES)
        acc[j % 2] = acc[j % 2] + w_ref[j] * e
    y_ref[pl.ds(out_row0 * groups, tb * groups), :] = (acc[0] + acc[1]).reshape(tb * groups, LANES)


def _conv_kernel(u_ref, w_ref, b_ref, y_ref, buf_out_ref, ext_scr, *, tile, tb, groups):
    t_idx = pl.program_id(1)
    hist = CONV_HIST * groups

    @pl.when(t_idx == 0)
    def _():
        ext_scr[0:hist, :] = jnp.zeros((hist, LANES), F32)

    @pl.when(t_idx > 0)
    def _():
        ext_scr[0:hist, :] = ext_scr[tile * groups:tile * groups + hist, :]

    ext_scr[hist:hist + tile * groups, :] = u_ref[...]

    def block_body(i, carry):
        r0 = pl.multiple_of(i * tb, tb)
        _conv_taps(ext_scr, w_ref, b_ref, y_ref, ext_row0=r0, out_row0=r0, tb=tb, groups=groups)
        return carry

    lax.fori_loop(0, tile // tb, block_body, 0)

    @pl.when(t_idx == pl.num_programs(1) - 1)
    def _():
        for g in range(groups):
            buf_out_ref[:, g * LANES:(g + 1) * LANES] = ext_scr[
                pl.ds(tile * groups + g, CONV_HIST, stride=groups), :]


def _conv(u, dw_w, dw_b, *, n, tile, tb):
    ch = dw_w.shape[1]
    groups = ch // LANES
    l = u.shape[0] // (n * groups)
    steps = l // tile
    blk = pl.BlockSpec((tile * groups, LANES), lambda i, t: (i * steps + t, 0))
    return pl.pallas_call(
        functools.partial(_conv_kernel, tile=tile, tb=tb, groups=groups),
        grid=(n, steps),
        in_specs=[blk, _const_spec((CONV_WIDTH, groups, LANES)), _const_spec((1, groups, LANES))],
        out_specs=[blk, pl.BlockSpec((None, CONV_HIST, ch), lambda i, t: (i, 0, 0))],
        out_shape=[jax.ShapeDtypeStruct(u.shape, F32), jax.ShapeDtypeStruct((n, CONV_HIST, ch), F32)],
        scratch_shapes=[pltpu.VMEM(((tile + CONV_HIST) * groups, LANES), F32)],
        compiler_params=_params("parallel", "arbitrary"),
        name="conv",
    )(u, dw_w.reshape(CONV_WIDTH, groups, LANES), dw_b.reshape(1, groups, LANES))


def _conv_step_kernel(u_ref, w_ref, b_ref, buf_ref, y_ref, buf_out_ref, ext_scr, *, steps, seqs, groups):
    hist = CONV_HIST * groups
    rows_seq = steps * groups
    for s in range(seqs):
        for g in range(groups):
            ext_scr[pl.ds(g, CONV_HIST, stride=groups), :] = buf_ref[s, :, g * LANES:(g + 1) * LANES]
        ext_scr[hist:hist + rows_seq, :] = u_ref[s * rows_seq:(s + 1) * rows_seq, :]
        _conv_taps(ext_scr, w_ref, b_ref, y_ref, ext_row0=0, out_row0=s * steps, tb=steps, groups=groups)
        for g in range(groups):
            buf_out_ref[s, :, g * LANES:(g + 1) * LANES] = ext_scr[
                pl.ds(rows_seq + g, CONV_HIST, stride=groups), :]


def _conv_step(u, dw_w, dw_b, buf, *, steps, seqs):
    n, _, ch = buf.shape
    groups = ch // LANES
    blk = pl.BlockSpec((seqs * steps * groups, LANES), lambda i: (i, 0))
    buf_spec = pl.BlockSpec((seqs, CONV_HIST, ch), lambda i: (i, 0, 0))
    return pl.pallas_call(
        functools.partial(_conv_step_kernel, steps=steps, seqs=seqs, groups=groups),
        grid=(n // seqs,),
        in_specs=[blk, _const_spec((CONV_WIDTH, groups, LANES)), _const_spec((1, groups, LANES)), buf_spec],
        out_specs=[blk, buf_spec],
        out_shape=[jax.ShapeDtypeStruct(u.shape, F32), jax.ShapeDtypeStruct(buf.shape, F32)],
        scratch_shapes=[pltpu.VMEM(((steps + CONV_HIST) * groups, LANES), F32)],
        compiler_params=_params("parallel"),
        name="conv_step",
    )(u, dw_w.reshape(CONV_WIDTH, groups, LANES), dw_b.reshape(1, groups, LANES), buf)


def _post_kernel(x_ref, oa_ref, yc_ref, sgb_ref, g1_ref, sh2_ref, sc2_ref, g2_ref, lnw_ref, lnb_ref,
                 wo_ref, nfw_ref, w1_ref, w2_ref, nlw_ref, y_ref, m_scr, c_scr, h_scr, acc_scr):
    tm, d = x_ref.shape
    groups = d // LANES
    fc = w2_ref.shape[1]
    for g in range(groups):
        rows = _group_rows(g, tm, groups)
        cols = slice(g * LANES, (g + 1) * LANES)
        m_scr[:, cols] = oa_ref[rows, :].astype(BF16)
        c_scr[:, cols] = yc_ref[rows, :]
    yc = c_scr[...]
    mu = jnp.mean(yc, axis=-1, keepdims=True)
    dv = yc - mu
    var = jnp.mean(dv * dv, axis=-1, keepdims=True)
    ob = _silu(dv * lax.rsqrt(var + EPS) * lnw_ref[...] + lnb_ref[...])
    for g in range(groups):
        cols = slice(g * LANES, (g + 1) * LANES)
        m_scr[:, d + g * LANES:d + (g + 1) * LANES] = (
            ob[:, cols] * sgb_ref[_group_rows(g, tm, groups), :]).astype(BF16)
    merged = jnp.dot(m_scr[...], wo_ref[...], preferred_element_type=F32)
    x1 = x_ref[...] + g1_ref[...] * merged
    ms = jnp.mean(x1 * x1, axis=-1, keepdims=True)
    h2 = x1 * lax.rsqrt(ms + EPS) * nfw_ref[...]
    h2 = h2 * (1.0 + sc2_ref[...]) + sh2_ref[...]
    h_scr[...] = h2.astype(BF16)
    acc_scr[...] = jnp.zeros_like(acc_scr)

    def ffn_body(c, carry):
        gu = jnp.dot(h_scr[...], w1_ref[c], preferred_element_type=F32)
        a = _silu(gu[:, :fc]) * gu[:, fc:]
        acc_scr[...] += jnp.dot(a.astype(BF16), w2_ref[c], preferred_element_type=F32)
        return carry

    lax.fori_loop(0, w1_ref.shape[0], ffn_body, 0)
    x2 = x1 + g2_ref[...] * acc_scr[...]
    ms2 = jnp.mean(x2 * x2, axis=-1, keepdims=True)
    y_ref[...] = x2 * lax.rsqrt(ms2 + EPS) * nlw_ref[...]


def _post(x, oa, yc, sgb, mod, lnw, lnb, wo, nfw, w1_r, w2_r, nlw, *, tm, rows_per_mod):
    m, d = x.shape
    groups = d // LANES
    r = mod.shape[2]
    tiles_per_mod = rows_per_mod // tm
    mod_spec = lambda row: pl.BlockSpec((None, None, r, d), lambda i: (i // tiles_per_mod, row, 0, 0))
    row_spec = pl.BlockSpec((tm, d), lambda i: (i, 0))
    il_spec = pl.BlockSpec((tm * groups, LANES), lambda i: (i, 0))
    vec = _const_spec((1, d))
    return pl.pallas_call(
        _post_kernel,
        grid=(m // tm,),
        in_specs=[row_spec, il_spec, il_spec, il_spec, mod_spec(2), mod_spec(3), mod_spec(4), mod_spec(5),
                  vec, vec, _const_spec(wo.shape), vec, _const_spec(w1_r.shape), _const_spec(w2_r.shape), vec],
        out_specs=row_spec,
        out_shape=jax.ShapeDtypeStruct((m, d), F32),
        scratch_shapes=[pltpu.VMEM((tm, 2 * d), BF16), pltpu.VMEM((tm, d), F32),
                        pltpu.VMEM((tm, d), BF16), pltpu.VMEM((tm, d), F32)],
        compiler_params=_params("parallel"),
        name="post",
    )(x, oa, yc, sgb, mod, mod, mod, mod, lnw, lnb, wo, nfw, w1_r, w2_r, nlw)


INPROJ_TC = 256
FFN_CHUNK = 256
PROMPT_TM = 512
PROMPT_CHUNK = 128
PROMPT_CONV_TILE = 256
PROMPT_CONV_TB = 16
SAMPLE_SEQS = 8
SAMPLE_PAD = 16


def kernel(x_prompt, x_sample, state_hgrn, state_conv, c_prompt, c_sample, w_ada, b_ada, norm_mix_w, w_in, b_in, lb_logits, hgrn_norm_w, conv_dw_w, conv_dw_b, conv_ln_w, conv_ln_b, w_out, norm_ffn_w, w_ffn_in, w_ffn_out, norm_final_w):
    depth = w_in.shape[0]
    assert depth == 1, "single-layer configuration"
    layer = 0
    n_p, l_p, d = x_prompt.shape
    n_s, l_s, _ = x_sample.shape
    heads = d // HEAD_DIM
    d_ff = w_ffn_out.shape[1]

    nc = d // INPROJ_TC
    w_in_r = (w_in[layer].astype(BF16).reshape(d, IN_SEGMENTS, nc, INPROJ_TC)
              .transpose(2, 0, 1, 3).reshape(nc, d, IN_SEGMENTS * INPROJ_TC))
    b_in_r = (b_in[layer].reshape(IN_SEGMENTS, nc, INPROJ_TC).transpose(1, 0, 2)
              .reshape(nc, 1, IN_SEGMENTS * INPROJ_TC))
    lbl_r = lb_logits.reshape(depth + 1, nc, INPROJ_TC).transpose(1, 0, 2)
    nf = d_ff // FFN_CHUNK
    w1 = w_ffn_in[layer].astype(BF16)
    w1_r = jnp.concatenate([w1[:, :d_ff].reshape(d, nf, FFN_CHUNK), w1[:, d_ff:].reshape(d, nf, FFN_CHUNK)],
                           axis=2).transpose(1, 0, 2)
    w2_r = w_ffn_out[layer].astype(BF16).reshape(nf, FFN_CHUNK, d)
    wo = w_out[layer].astype(BF16)
    row = lambda a: a.reshape(1, d)

    mod = _mod(jnp.concatenate([c_prompt, c_sample], axis=0), w_ada[layer], b_ada[layer])
    mod = mod.reshape(n_p + n_s, MOD_ROWS, d)
    mod_p = mod[:n_p].reshape(n_p, MOD_ROWS, 1, d)
    mod_s = jnp.broadcast_to(mod[n_p:, None, :, :], (n_s, l_s, MOD_ROWS, d))
    mod_s = mod_s.transpose(2, 0, 1, 3).reshape(1, MOD_ROWS, n_s * l_s, d)

    def inproj(x2, mod_g, tm):
        return _inproj(x2, mod_g, row(norm_mix_w[layer]), w_in_r, b_in_r, lbl_r,
                       layer=layer, tm=tm, rows_per_mod=tm if mod_g.shape[2] > 1 else l_p)

    def post(x2, oa, yc, sgb, mod_g, tm):
        return _post(x2, oa, yc, sgb, mod_g, row(conv_ln_w[layer]), row(conv_ln_b[layer]), wo,
                     row(norm_ffn_w[layer]), w1_r, w2_r, row(norm_final_w),
                     tm=tm, rows_per_mod=tm if mod_g.shape[2] > 1 else l_p)

    xp = x_prompt.reshape(n_p * l_p, d)
    q, f, v, ga, u, sgb = inproj(xp, mod_p, PROMPT_TM)
    oa, s_p = _recur(q, f, v, ga, hgrn_norm_w[layer], n=n_p, heads=heads, chunk=PROMPT_CHUNK)
    yc, b_p = _conv(u, conv_dw_w[layer], conv_dw_b[layer], n=n_p, tile=PROMPT_CONV_TILE, tb=PROMPT_CONV_TB)
    y_p = post(xp, oa, yc, sgb, mod_p, PROMPT_TM).reshape(n_p, l_p, d)

    xs = x_sample.reshape(n_s * l_s, d)
    q, f, v, ga, u, sgb = inproj(xs, mod_s, n_s * l_s)
    oa, s_s = _recur_step(q, f, v, ga, hgrn_norm_w[layer], state_hgrn[layer],
                          heads=heads, steps=l_s, seqs=SAMPLE_SEQS, pad=SAMPLE_PAD)
    yc, b_s = _conv_step(u, conv_dw_w[layer], conv_dw_b[layer], state_conv[layer], steps=l_s, seqs=SAMPLE_SEQS)
    y_s = post(xs, oa, yc, sgb, mod_s, n_s * l_s).reshape(n_s, l_s, d)

    return (y_p, y_s, s_p[None], b_p[None], s_s[None], b_s[None])
```

```python
import functools

import numpy as np
import jax
import jax.numpy as jnp
from jax import lax
from jax.experimental import pallas as pl
from jax.experimental.pallas import tpu as pltpu

F32 = jnp.float32
BF16 = jnp.bfloat16

EPS = 1e-6
LANES = 128
SUBLANES = 8
HEAD_DIM = LANES
CONV_WIDTH = 31
CONV_HIST = CONV_WIDTH - 1
VMEM_LIMIT_BYTES = 56 * 1024 * 1024

IN_SEGMENTS = 8
MOD_ROWS = 6

NT = (((1,), (1,)), ((), ()))
TN = (((0,), (0,)), ((), ()))


def _silu(x):
    return x * jax.nn.sigmoid(x)


def _params(*sem):
    return pltpu.CompilerParams(dimension_semantics=sem, vmem_limit_bytes=VMEM_LIMIT_BYTES)


def _const_spec(shape):
    nd = len(shape)
    return pl.BlockSpec(shape, lambda *_: (0,) * nd, pipeline_mode=pl.Buffered(1))


def _group_rows(g, n, groups):
    return pl.ds(g, n, stride=groups)


def _mod_kernel(c_ref, w_ref, b_ref, o_ref):
    a = _silu(c_ref[...]).astype(BF16)
    o_ref[...] = jnp.dot(a, w_ref[...].astype(BF16), preferred_element_type=F32) + b_ref[...]


def _mod(c, w_ada, b_ada, *, tn=512):
    n, d = c.shape
    cols = w_ada.shape[1]
    return pl.pallas_call(
        _mod_kernel,
        grid=(cols // tn,),
        in_specs=[pl.BlockSpec((n, d), lambda j: (0, 0)),
                  pl.BlockSpec((d, tn), lambda j: (0, j)),
                  pl.BlockSpec((1, tn), lambda j: (0, j))],
        out_specs=pl.BlockSpec((n, tn), lambda j: (0, j)),
        out_shape=jax.ShapeDtypeStruct((n, cols), F32),
        compiler_params=_params("parallel"),
        name="mod",
    )(c, w_ada, b_ada.reshape(1, cols))


def _inproj_kernel(x_ref, sh_ref, sc_ref, nw_ref, w_ref, b_ref, lbl_ref,
                   q_ref, f_ref, v_ref, ga_ref, u_ref, sgb_ref, h_scr, *, layer, tc):
    tm, d = x_ref.shape
    groups = d // LANES
    x = x_ref[...]
    ms = jnp.mean(x * x, axis=-1, keepdims=True)
    h = x * lax.rsqrt(ms + EPS) * nw_ref[...]
    h = h * (1.0 + sc_ref[...]) + sh_ref[...]
    h_scr[...] = h.astype(BF16)
    for j in range(d // tc):
        seg = []
        for s in range(IN_SEGMENTS):
            cols = slice(s * d + j * tc, s * d + (j + 1) * tc)
            seg.append(jnp.dot(h_scr[...], w_ref[:, cols], preferred_element_type=F32) + b_ref[:, cols])
        q_raw, f_raw, i_raw, og_raw, glu_a, glu_b, gate_a, gate_b = seg
        lg = lbl_ref[:, j * tc:(j + 1) * tc]
        e = jnp.exp(lg - jnp.max(lg, axis=0, keepdims=True))
        lb = jnp.sum(e[:layer + 1], axis=0, keepdims=True) / jnp.sum(e, axis=0, keepdims=True)
        outs = ((q_ref, _silu(q_raw)),
                (f_ref, lb + (1.0 - lb) * jax.nn.sigmoid(f_raw)),
                (v_ref, i_raw),
                (ga_ref, _silu(og_raw) * jax.nn.sigmoid(gate_a)),
                (u_ref, glu_a * jax.nn.sigmoid(glu_b)),
                (sgb_ref, jax.nn.sigmoid(gate_b)))
        for ref, val in outs:
            for g in range(tc // LANES):
                rows = _group_rows(j * (tc // LANES) + g, tm, groups)
                ref[rows, :] = val[:, g * LANES:(g + 1) * LANES]


def _mod_spec(mod, row, d, tm, rows_per_mod):
    tiles_per_mod = rows_per_mod // tm
    return pl.BlockSpec((None, mod.shape[1], d), lambda i: (i // tiles_per_mod, 0, row))


def _inproj(x, mod, nw, w, b, lbl, *, layer, tm, tc, rows_per_mod):
    m, d = x.shape
    groups = d // LANES
    mod_spec = lambda row: _mod_spec(mod, row, d, tm, rows_per_mod)
    out_spec = pl.BlockSpec((tm * groups, LANES), lambda i: (i, 0))
    out_shape = jax.ShapeDtypeStruct((m * groups, LANES), F32)
    return pl.pallas_call(
        functools.partial(_inproj_kernel, layer=layer, tc=tc),
        grid=(m // tm,),
        in_specs=[pl.BlockSpec((tm, d), lambda i: (i, 0)), mod_spec(0), mod_spec(1),
                  _const_spec((1, d)), _const_spec(w.shape), _const_spec(b.shape),
                  _const_spec(lbl.shape)],
        out_specs=[out_spec] * 6,
        out_shape=[out_shape] * 6,
        scratch_shapes=[pltpu.VMEM((tm, d), BF16)],
        compiler_params=_params("parallel"),
        name="inproj",
    )(x, mod, mod, nw, w, b, lbl)


def _split_level_table(c):
    t = np.arange(c)[:, None]
    s = np.arange(c)[None, :]
    x = t ^ s
    lvl = np.zeros((c, c), np.int32)
    for l in range(1, c.bit_length()):
        lvl[(x >> (l - 1)) == 1] = l
    return np.where(t > s, lvl, 0).astype(np.int32)


def _recur_kernel(q_ref, f_ref, v_ref, ga_ref, nw_ref, lvl_ref, o_ref, s_out_ref,
                  st_scr, x_scr, d_scr, *, heads, chunk):
    c_idx = pl.program_id(1)
    nlev = chunk.bit_length() - 1

    @pl.when(c_idx == 0)
    def _():
        st_scr[...] = jnp.zeros_like(st_scr)

    def to3(a):
        return a.reshape(chunk, heads, HEAD_DIM)

    def flat(a):
        return a.reshape(chunk * heads, HEAD_DIM)

    f3 = to3(f_ref[...])
    q3 = to3(q_ref[...])
    v3 = to3(v_ref[...])
    k3 = 1.0 - f3
    p, r = f3, jnp.ones_like(f3)
    half = 1
    for l in range(nlev):
        blk = 2 * half
        shp = (chunk // blk, blk, heads, HEAD_DIM)
        p4, r4, q4, k4 = (a.reshape(shp) for a in (p, r, q3, k3))
        x = jnp.concatenate([k4[:, :half] * r4[:, :half], q4[:, half:] * p4[:, half:]], axis=1)
        x_scr[l] = flat(x)
        left_tot = p4[:, half - 1:half]
        right_tot = p4[:, blk - 1:blk]
        p = jnp.concatenate([p4[:, :half], p4[:, half:] * left_tot], axis=1).reshape(f3.shape)
        r = jnp.concatenate([r4[:, :half] * right_tot, r4[:, half:]], axis=1).reshape(f3.shape)
        half = blk
    x_scr[nlev] = flat(q3 * p)
    x_scr[nlev + 1] = flat(k3 * r)
    d_scr[...] = flat(jnp.sum(q3 * k3, axis=-1, keepdims=True) * v3)
    p_last = p[chunk - 1]

    lvl = lvl_ref[...]
    for h in range(heads):
        rows = _group_rows(h, chunk, heads)
        att = jnp.zeros((chunk, chunk), F32)
        for l in range(nlev):
            xb = x_scr[l, rows, :].astype(BF16)
            a = lax.dot_general(xb, xb, NT, preferred_element_type=F32)
            att = jnp.where(lvl == l + 1, a, att)
        qt = x_scr[nlev, rows, :].astype(BF16)
        kt = x_scr[nlev + 1, rows, :].astype(BF16)
        vb = v_ref[rows, :].astype(BF16)
        st0 = st_scr[h]
        o = lax.dot_general(qt, st0.astype(BF16), NT, preferred_element_type=F32)
        o = o + jnp.dot(att.astype(BF16), vb, preferred_element_type=F32)
        o_ref[rows, :] = o + d_scr[rows, :]
        st_scr[h] = st0 * p_last[h:h + 1, :] + lax.dot_general(vb, kt, TN, preferred_element_type=F32)

    o3 = to3(o_ref[...])
    o3 = o3 * lax.rsqrt(jnp.mean(o3 * o3, axis=-1, keepdims=True) + EPS)
    o_ref[...] = flat(o3 * nw_ref[...] * to3(ga_ref[...]))

    @pl.when(c_idx == pl.num_programs(1) - 1)
    def _():
        for h in range(heads):
            s_out_ref[h] = st_scr[h].T


def _recur(q, f, v, ga, norm_w, *, n, heads, chunk):
    rows_total = q.shape[0]
    l = rows_total // (n * heads)
    nlev = chunk.bit_length() - 1
    rows = chunk * heads
    steps = l // chunk
    blk = pl.BlockSpec((rows, HEAD_DIM), lambda i, c: (i * steps + c, 0))
    st_spec = pl.BlockSpec((None, heads, HEAD_DIM, HEAD_DIM), lambda i, c: (i, 0, 0, 0))
    lvl = jnp.asarray(_split_level_table(chunk))
    return pl.pallas_call(
        functools.partial(_recur_kernel, heads=heads, chunk=chunk),
        grid=(n, steps),
        in_specs=[blk, blk, blk, blk, _const_spec((heads, HEAD_DIM)), _const_spec((chunk, chunk))],
        out_specs=[blk, st_spec],
        out_shape=[jax.ShapeDtypeStruct((rows_total, HEAD_DIM), F32),
                   jax.ShapeDtypeStruct((n, heads, HEAD_DIM, HEAD_DIM), F32)],
        scratch_shapes=[pltpu.VMEM((heads, HEAD_DIM, HEAD_DIM), F32),
                        pltpu.VMEM((nlev + 2, rows, HEAD_DIM), F32),
                        pltpu.VMEM((rows, HEAD_DIM), F32)],
        compiler_params=_params("parallel", "arbitrary"),
        name="recur",
    )(q, f, v, ga, norm_w.reshape(heads, HEAD_DIM), lvl)


def _recur_step_kernel(q_ref, f_ref, v_ref, ga_ref, nw_ref, s0_ref, o_ref, s_out_ref,
                       qt_scr, kt_scr, vp_scr, oi_scr, pt_scr, *, heads, steps, seqs, pad):
    rows_seq = steps * heads
    zero = jnp.zeros((pad * heads, HEAD_DIM), F32)
    qt_scr[...] = zero
    kt_scr[...] = zero
    vp_scr[...] = zero
    pt_scr[...] = jnp.zeros_like(pt_scr)

    def seq_body(g, carry):
        base = pl.multiple_of(g * rows_seq, rows_seq)
        blk = pl.ds(base, rows_seq)
        to3 = lambda a: a.reshape(steps, heads, HEAD_DIM)
        q3, f3, v3 = to3(q_ref[blk, :]), to3(f_ref[blk, :]), to3(v_ref[blk, :])
        k3 = 1.0 - f3
        p = [f3[0]]
        for t in range(1, steps):
            p.append(p[-1] * f3[t])
        r = [None] * steps
        r[steps - 1] = jnp.ones_like(f3[0])
        for s in range(steps - 2, -1, -1):
            r[s] = r[s + 1] * f3[s + 1]
        intra = []
        for t in range(steps):
            acc = jnp.sum(q3[t] * k3[t], axis=-1, keepdims=True) * v3[t]
            dec = None
            for s in range(t - 1, -1, -1):
                dec = f3[s + 1] if dec is None else dec * f3[s + 1]
                acc = acc + jnp.sum(q3[t] * k3[s] * dec, axis=-1, keepdims=True) * v3[s]
            intra.append(acc)
        for t in range(steps):
            rows_t = pl.ds(t * heads, heads)
            qt_scr[rows_t, :] = q3[t] * p[t]
            kt_scr[rows_t, :] = k3[t] * r[t]
            vp_scr[rows_t, :] = v3[t]
        pt_scr[0:heads, :] = p[steps - 1]
        p_col = pt_scr[...].T
        for h in range(heads):
            rows = _group_rows(h, pad, heads)
            qt = qt_scr[rows, :].astype(BF16)
            kt = kt_scr[rows, :].astype(BF16)
            vb = vp_scr[rows, :].astype(BF16)
            s0 = s0_ref[g, h]
            oi_scr[rows, :] = jnp.dot(qt, s0.astype(BF16), preferred_element_type=F32)
            s_out_ref[g, h] = s0 * p_col[:, h:h + 1] + lax.dot_general(
                kt, vb, TN, preferred_element_type=F32)
        o3 = to3(oi_scr[0:rows_seq, :]) + jnp.stack(intra, axis=0)
        o3 = o3 * lax.rsqrt(jnp.mean(o3 * o3, axis=-1, keepdims=True) + EPS)
        o_ref[blk, :] = (o3 * nw_ref[...] * to3(ga_ref[blk, :])).reshape(rows_seq, HEAD_DIM)
        return carry

    lax.fori_loop(0, seqs, seq_body, 0)


def _recur_step(q, f, v, ga, norm_w, s0, *, heads, steps, seqs, pad):
    n = s0.shape[0]
    rows = seqs * steps * heads
    blk = pl.BlockSpec((rows, HEAD_DIM), lambda i: (i, 0))
    st_spec = pl.BlockSpec((seqs, heads, HEAD_DIM, HEAD_DIM), lambda i: (i, 0, 0, 0))
    scr = pltpu.VMEM((pad * heads, HEAD_DIM), F32)
    return pl.pallas_call(
        functools.partial(_recur_step_kernel, heads=heads, steps=steps, seqs=seqs, pad=pad),
        grid=(n // seqs,),
        in_specs=[blk, blk, blk, blk, _const_spec((heads, HEAD_DIM)), st_spec],
        out_specs=[blk, st_spec],
        out_shape=[jax.ShapeDtypeStruct(q.shape, F32), jax.ShapeDtypeStruct(s0.shape, F32)],
        scratch_shapes=[scr, scr, scr, scr, pltpu.VMEM((HEAD_DIM, HEAD_DIM), F32)],
        compiler_params=_params("parallel"),
        name="recur_step",
    )(q, f, v, ga, norm_w.reshape(heads, HEAD_DIM), s0)


def _conv_taps(ext_scr, w_ref, b_ref, y_ref, *, ext_row0, out_row0, tb, groups):
    acc = [jnp.broadcast_to(b_ref[...], (tb, groups, LANES)), jnp.zeros((tb, groups, LANES), F32)]
    for j in range(CONV_WIDTH):
        start = (ext_row0 + j) * groups
        if not isinstance(start, int):
            start = pl.multiple_of(start, groups)
        e = ext_scr[pl.ds(start, tb * groups), :].reshape(tb, groups, LANES)
        acc[j % 2] = acc[j % 2] + w_ref[j] * e
    y_ref[pl.ds(out_row0 * groups, tb * groups), :] = (acc[0] + acc[1]).reshape(tb * groups, LANES)


def _conv_kernel(u_ref, w_ref, b_ref, y_ref, buf_out_ref, ext_scr, *, tile, tb, groups):
    t_idx = pl.program_id(1)
    hist = CONV_HIST * groups

    @pl.when(t_idx == 0)
    def _():
        ext_scr[0:hist, :] = jnp.zeros((hist, LANES), F32)

    @pl.when(t_idx > 0)
    def _():
        ext_scr[0:hist, :] = ext_scr[tile * groups:tile * groups + hist, :]

    ext_scr[hist:hist + tile * groups, :] = u_ref[...]

    def block_body(i, carry):
        r0 = pl.multiple_of(i * tb, tb)
        _conv_taps(ext_scr, w_ref, b_ref, y_ref, ext_row0=r0, out_row0=r0, tb=tb, groups=groups)
        return carry

    lax.fori_loop(0, tile // tb, block_body, 0)

    @pl.when(t_idx == pl.num_programs(1) - 1)
    def _():
        for g in range(groups):
            buf_out_ref[:, g * LANES:(g + 1) * LANES] = ext_scr[
                pl.ds(tile * groups + g, CONV_HIST, stride=groups), :]


def _conv(u, dw_w, dw_b, *, n, tile, tb):
    ch = dw_w.shape[1]
    groups = ch // LANES
    l = u.shape[0] // (n * groups)
    steps = l // tile
    blk = pl.BlockSpec((tile * groups, LANES), lambda i, t: (i * steps + t, 0))
    return pl.pallas_call(
        functools.partial(_conv_kernel, tile=tile, tb=tb, groups=groups),
        grid=(n, steps),
        in_specs=[blk, _const_spec((CONV_WIDTH, groups, LANES)), _const_spec((1, groups, LANES))],
        out_specs=[blk, pl.BlockSpec((None, CONV_HIST, ch), lambda i, t: (i, 0, 0))],
        out_shape=[jax.ShapeDtypeStruct(u.shape, F32), jax.ShapeDtypeStruct((n, CONV_HIST, ch), F32)],
        scratch_shapes=[pltpu.VMEM(((tile + CONV_HIST) * groups, LANES), F32)],
        compiler_params=_params("parallel", "arbitrary"),
        name="conv",
    )(u, dw_w.reshape(CONV_WIDTH, groups, LANES), dw_b.reshape(1, groups, LANES))


def _conv_step_kernel(u_ref, w_ref, b_ref, buf_ref, y_ref, buf_out_ref, ext_scr, *, steps, seqs, groups):
    hist = CONV_HIST * groups
    rows_seq = steps * groups
    for s in range(seqs):
        for g in range(groups):
            ext_scr[pl.ds(g, CONV_HIST, stride=groups), :] = buf_ref[s, :, g * LANES:(g + 1) * LANES]
        ext_scr[hist:hist + rows_seq, :] = u_ref[s * rows_seq:(s + 1) * rows_seq, :]
        _conv_taps(ext_scr, w_ref, b_ref, y_ref, ext_row0=0, out_row0=s * steps, tb=steps, groups=groups)
        for g in range(groups):
            buf_out_ref[s, :, g * LANES:(g + 1) * LANES] = ext_scr[
                pl.ds(rows_seq + g, CONV_HIST, stride=groups), :]


def _conv_step(u, dw_w, dw_b, buf, *, steps, seqs):
    n, _, ch = buf.shape
    groups = ch // LANES
    blk = pl.BlockSpec((seqs * steps * groups, LANES), lambda i: (i, 0))
    buf_spec = pl.BlockSpec((seqs, CONV_HIST, ch), lambda i: (i, 0, 0))
    return pl.pallas_call(
        functools.partial(_conv_step_kernel, steps=steps, seqs=seqs, groups=groups),
        grid=(n // seqs,),
        in_specs=[blk, _const_spec((CONV_WIDTH, groups, LANES)), _const_spec((1, groups, LANES)), buf_spec],
        out_specs=[blk, buf_spec],
        out_shape=[jax.ShapeDtypeStruct(u.shape, F32), jax.ShapeDtypeStruct(buf.shape, F32)],
        scratch_shapes=[pltpu.VMEM(((steps + CONV_HIST) * groups, LANES), F32)],
        compiler_params=_params("parallel"),
        name="conv_step",
    )(u, dw_w.reshape(CONV_WIDTH, groups, LANES), dw_b.reshape(1, groups, LANES), buf)


def _post_kernel(x_ref, oa_ref, yc_ref, sgb_ref, g1_ref, sh2_ref, sc2_ref, g2_ref, lnw_ref, lnb_ref,
                 wo_ref, nfw_ref, w1_ref, w2_ref, nlw_ref, y_ref, m_scr, c_scr, h_scr, acc_scr, *, fc):
    tm, d = x_ref.shape
    groups = d // LANES
    for g in range(groups):
        rows = _group_rows(g, tm, groups)
        cols = slice(g * LANES, (g + 1) * LANES)
        m_scr[:, cols] = oa_ref[rows, :].astype(BF16)
        c_scr[:, cols] = yc_ref[rows, :]
    yc = c_scr[...]
    mu = jnp.mean(yc, axis=-1, keepdims=True)
    dv = yc - mu
    var = jnp.mean(dv * dv, axis=-1, keepdims=True)
    ob = _silu(dv * lax.rsqrt(var + EPS) * lnw_ref[...] + lnb_ref[...])
    for g in range(groups):
        cols = slice(g * LANES, (g + 1) * LANES)
        m_scr[:, d + g * LANES:d + (g + 1) * LANES] = (
            ob[:, cols] * sgb_ref[_group_rows(g, tm, groups), :]).astype(BF16)
    merged = jnp.dot(m_scr[...], wo_ref[...], preferred_element_type=F32)
    x1 = x_ref[...] + g1_ref[...] * merged
    ms = jnp.mean(x1 * x1, axis=-1, keepdims=True)
    h2 = x1 * lax.rsqrt(ms + EPS) * nfw_ref[...]
    h2 = h2 * (1.0 + sc2_ref[...]) + sh2_ref[...]
    h_scr[...] = h2.astype(BF16)
    d_ff = w2_ref.shape[0]
    for c in range(d_ff // fc):
        gate = jnp.dot(h_scr[...], w1_ref[:, c * fc:(c + 1) * fc], preferred_element_type=F32)
        up = jnp.dot(h_scr[...], w1_ref[:, d_ff + c * fc:d_ff + (c + 1) * fc], preferred_element_type=F32)
        part = jnp.dot((_silu(gate) * up).astype(BF16), w2_ref[c * fc:(c + 1) * fc, :],
                       preferred_element_type=F32)
        if c == 0:
            acc_scr[...] = part
        else:
            acc_scr[...] += part
    x2 = x1 + g2_ref[...] * acc_scr[...]
    ms2 = jnp.mean(x2 * x2, axis=-1, keepdims=True)
    y_ref[...] = x2 * lax.rsqrt(ms2 + EPS) * nlw_ref[...]


def _post(x, oa, yc, sgb, mod, lnw, lnb, wo, nfw, w1, w2, nlw, *, tm, fc, rows_per_mod):
    m, d = x.shape
    groups = d // LANES
    mod_spec = lambda row: _mod_spec(mod, row, d, tm, rows_per_mod)
    row_spec = pl.BlockSpec((tm, d), lambda i: (i, 0))
    il_spec = pl.BlockSpec((tm * groups, LANES), lambda i: (i, 0))
    vec = _const_spec((1, d))
    return pl.pallas_call(
        functools.partial(_post_kernel, fc=fc),
        grid=(m // tm,),
        in_specs=[row_spec, il_spec, il_spec, il_spec, mod_spec(2), mod_spec(3), mod_spec(4), mod_spec(5),
                  vec, vec, _const_spec(wo.shape), vec, _const_spec(w1.shape), _const_spec(w2.shape), vec],
        out_specs=row_spec,
        out_shape=jax.ShapeDtypeStruct((m, d), F32),
        scratch_shapes=[pltpu.VMEM((tm, 2 * d), BF16), pltpu.VMEM((tm, d), F32),
                        pltpu.VMEM((tm, d), BF16), pltpu.VMEM((tm, d), F32)],
        compiler_params=_params("parallel"),
        name="post",
    )(x, oa, yc, sgb, mod, mod, mod, mod, lnw, lnb, wo, nfw, w1, w2, nlw)


INPROJ_TC = 256
FFN_CHUNK = 256
PROMPT_TM = 512
PROMPT_CHUNK = 128
PROMPT_CONV_TILE = 256
PROMPT_CONV_TB = 16
SAMPLE_SEQS = 8
SAMPLE_PAD = 16


def kernel(x_prompt, x_sample, state_hgrn, state_conv, c_prompt, c_sample, w_ada, b_ada, norm_mix_w, w_in, b_in, lb_logits, hgrn_norm_w, conv_dw_w, conv_dw_b, conv_ln_w, conv_ln_b, w_out, norm_ffn_w, w_ffn_in, w_ffn_out, norm_final_w):
    depth = w_in.shape[0]
    assert depth == 1, "single-layer configuration"
    layer = 0
    n_p, l_p, d = x_prompt.shape
    n_s, l_s, _ = x_sample.shape
    heads = d // HEAD_DIM
    d_ff = w_ffn_out.shape[1]

    w_in_b = w_in[layer].astype(BF16)
    w1 = w_ffn_in[layer].astype(BF16)
    w2 = w_ffn_out[layer].astype(BF16)
    wo = w_out[layer].astype(BF16)
    row = lambda a: a.reshape(1, -1)

    mod = _mod(jnp.concatenate([c_prompt, c_sample], axis=0), w_ada[layer], b_ada[layer])
    mod_p = mod[:n_p].reshape(n_p, 1, MOD_ROWS * d)
    mod_s = jnp.repeat(mod[n_p:], l_s, axis=0).reshape(1, n_s * l_s, MOD_ROWS * d)

    def inproj(x2, mod_g, tm):
        return _inproj(x2, mod_g, row(norm_mix_w[layer]), w_in_b, row(b_in[layer]), lb_logits,
                       layer=layer, tm=tm, tc=INPROJ_TC, rows_per_mod=tm if mod_g.shape[1] > 1 else l_p)

    def post(x2, oa, yc, sgb, mod_g, tm):
        return _post(x2, oa, yc, sgb, mod_g, row(conv_ln_w[layer]), row(conv_ln_b[layer]), wo,
                     row(norm_ffn_w[layer]), w1, w2, row(norm_final_w),
                     tm=tm, fc=FFN_CHUNK, rows_per_mod=tm if mod_g.shape[1] > 1 else l_p)

    xp = x_prompt.reshape(n_p * l_p, d)
    q, f, v, ga, u, sgb = inproj(xp, mod_p, PROMPT_TM)
    oa, s_p = _recur(q, f, v, ga, hgrn_norm_w[layer], n=n_p, heads=heads, chunk=PROMPT_CHUNK)
    yc, b_p = _conv(u, conv_dw_w[layer], conv_dw_b[layer], n=n_p, tile=PROMPT_CONV_TILE, tb=PROMPT_CONV_TB)
    y_p = post(xp, oa, yc, sgb, mod_p, PROMPT_TM).reshape(n_p, l_p, d)

    xs = x_sample.reshape(n_s * l_s, d)
    q, f, v, ga, u, sgb = inproj(xs, mod_s, n_s * l_s)
    oa, s_s = _recur_step(q, f, v, ga, hgrn_norm_w[layer], state_hgrn[layer],
                          heads=heads, steps=l_s, seqs=SAMPLE_SEQS, pad=SAMPLE_PAD)
    yc, b_s = _conv_step(u, conv_dw_w[layer], conv_dw_b[layer], state_conv[layer], steps=l_s, seqs=SAMPLE_SEQS)
    y_s = post(xs, oa, yc, sgb, mod_s, n_s * l_s).reshape(n_s, l_s, d)

    return (y_p, y_s, s_p[None], b_p[None], s_s[None], b_s[None])
```

```python
import functools

import numpy as np
import jax
import jax.numpy as jnp
from jax import lax
from jax.experimental import pallas as pl
from jax.experimental.pallas import tpu as pltpu

F32 = jnp.float32
BF16 = jnp.bfloat16

EPS = 1e-6
LANES = 128
SUBLANES = 8
HEAD_DIM = LANES
CONV_WIDTH = 31
CONV_HIST = CONV_WIDTH - 1
VMEM_LIMIT_BYTES = 56 * 1024 * 1024

IN_SEGMENTS = 8
MOD_ROWS = 6

NT = (((1,), (1,)), ((), ()))
TN = (((0,), (0,)), ((), ()))


def _silu(x):
    return x * jax.nn.sigmoid(x)


def _params(*sem):
    return pltpu.CompilerParams(dimension_semantics=sem, vmem_limit_bytes=VMEM_LIMIT_BYTES)


def _const_spec(shape):
    nd = len(shape)
    return pl.BlockSpec(shape, lambda *_: (0,) * nd, pipeline_mode=pl.Buffered(1))


def _zero_row(x):
    bits = lax.bitcast_convert_type(x[0:1, :], jnp.int32)
    return lax.shift_right_logical(lax.shift_right_logical(bits, 16), 16).astype(F32)


def _group_rows(g, n, groups):
    return pl.ds(g, n, stride=groups)


def _mod_kernel(c_ref, w_ref, b_ref, o_ref):
    a = _silu(c_ref[...]).astype(BF16)
    o_ref[...] = jnp.dot(a, w_ref[...].astype(BF16), preferred_element_type=F32) + b_ref[...]


def _mod(c, w_ada, b_ada, *, tn=512):
    n, d = c.shape
    cols = w_ada.shape[1]
    return pl.pallas_call(
        _mod_kernel,
        grid=(cols // tn,),
        in_specs=[pl.BlockSpec((n, d), lambda j: (0, 0)),
                  pl.BlockSpec((d, tn), lambda j: (0, j)),
                  pl.BlockSpec((1, tn), lambda j: (0, j))],
        out_specs=pl.BlockSpec((n, tn), lambda j: (0, j)),
        out_shape=jax.ShapeDtypeStruct((n, cols), F32),
        compiler_params=_params("parallel"),
        name="mod",
    )(c, w_ada, b_ada.reshape(1, cols))


def _project(x_ref, sh_ref, sc_ref, nw_ref, w_ref, b_ref, lbl_ref,
             q_ref, f_ref, v_ref, ga_ref, u_ref, sgb_ref, h_scr, *, layer, tc, side_work=None):
    tm, d = x_ref.shape
    groups = d // LANES
    n_chunks = d // tc
    x = x_ref[...]
    ms = jnp.mean(x * x, axis=-1, keepdims=True)
    h = x * lax.rsqrt(ms + EPS) * nw_ref[...]
    h = h * (1.0 + sc_ref[...]) + sh_ref[...]
    h_scr[...] = h.astype(BF16)
    done = [0]
    for j in range(n_chunks):
        def seg(s):
            cols = slice(s * d + j * tc, s * d + (j + 1) * tc)
            z = jnp.dot(h_scr[...], w_ref[:, cols], preferred_element_type=F32) + b_ref[:, cols]
            if side_work is not None:
                side = side_work(done[0], n_chunks * IN_SEGMENTS)
                if side is not None:
                    z = z + jnp.tile(_zero_row(side), (1, tc // LANES))
            done[0] += 1
            return z

        def put(ref, val):
            for g in range(tc // LANES):
                rows = _group_rows(j * (tc // LANES) + g, tm, groups)
                ref[rows, :] = val[:, g * LANES:(g + 1) * LANES]

        put(q_ref, _silu(seg(0)))
        lg = lbl_ref[:, j * tc:(j + 1) * tc]
        e = jnp.exp(lg - jnp.max(lg, axis=0, keepdims=True))
        lb = jnp.sum(e[:layer + 1], axis=0, keepdims=True) / jnp.sum(e, axis=0, keepdims=True)
        put(f_ref, lb + (1.0 - lb) * jax.nn.sigmoid(seg(1)))
        put(v_ref, seg(2))
        og_act = _silu(seg(3))
        put(ga_ref, og_act * jax.nn.sigmoid(seg(6)))
        glu_a = seg(4)
        put(u_ref, glu_a * jax.nn.sigmoid(seg(5)))
        put(sgb_ref, jax.nn.sigmoid(seg(7)))


def _inproj_kernel(*refs, layer, tc):
    _project(*refs, layer=layer, tc=tc)


def _inproj_conv_kernel(x_ref, sh_ref, sc_ref, nw_ref, w_ref, b_ref, lbl_ref, cw_ref, cb_ref,
                        q_ref, f_ref, v_ref, ga_ref, yc_ref, sgb_ref, buf_out_ref,
                        h_scr, ext_scr, unew_scr, *, layer, tc, tiles_per_seq, conv_tb):
    i = pl.program_id(0)
    tiles = pl.num_programs(0) - 1
    tm, d = x_ref.shape
    groups = d // LANES
    hist, tile_rows = CONV_HIST * groups, tm * groups
    n_conv = tm // conv_tb

    def conv_blocks(lo, hi):
        total = None
        for blk in range(lo, hi):
            y = _conv_taps(ext_scr, cw_ref, cb_ref, yc_ref, ext_row0=blk * conv_tb,
                           out_row0=blk * conv_tb, tb=conv_tb, groups=groups)
            part = jnp.sum(y, axis=0, keepdims=True)
            total = part if total is None else total + part
        return total

    @pl.when(i == 0)
    def _():
        ext_scr[...] = jnp.zeros_like(ext_scr)

    @pl.when((i > 0) & (i % tiles_per_seq == 0))
    def _():
        for g in range(groups):
            buf_out_ref[:, g * LANES:(g + 1) * LANES] = ext_scr[
                pl.ds(tile_rows + g, CONV_HIST, stride=groups), :]

    @pl.when(i < tiles)
    def _():
        _project(x_ref, sh_ref, sc_ref, nw_ref, w_ref, b_ref, lbl_ref,
                 q_ref, f_ref, v_ref, ga_ref, unew_scr, sgb_ref, h_scr, layer=layer, tc=tc,
                 side_work=lambda k, n: conv_blocks(k * n_conv // n, (k + 1) * n_conv // n))
        tail = ext_scr[tile_rows:tile_rows + hist, :]
        ext_scr[0:hist, :] = jnp.where(i % tiles_per_seq == 0, 0.0, tail)
        ext_scr[hist:hist + tile_rows, :] = unew_scr[...]

    @pl.when(i == tiles)
    def _():
        conv_blocks(0, n_conv)


def _mod_spec(mod, row, d, tm, rows_per_mod, tile_of=lambda i: i):
    tiles_per_mod = rows_per_mod // tm
    return pl.BlockSpec((None, mod.shape[1], d), lambda i: (tile_of(i) // tiles_per_mod, 0, row))


def _inproj(x, mod, nw, w, b, lbl, *, layer, tm, tc, rows_per_mod):
    m, d = x.shape
    groups = d // LANES
    mod_spec = lambda row: _mod_spec(mod, row, d, tm, rows_per_mod)
    out_spec = pl.BlockSpec((tm * groups, LANES), lambda i: (i, 0))
    out_shape = jax.ShapeDtypeStruct((m * groups, LANES), F32)
    return pl.pallas_call(
        functools.partial(_inproj_kernel, layer=layer, tc=tc),
        grid=(m // tm,),
        in_specs=[pl.BlockSpec((tm, d), lambda i: (i, 0)), mod_spec(0), mod_spec(1),
                  _const_spec((1, d)), _const_spec(w.shape), _const_spec(b.shape),
                  _const_spec(lbl.shape)],
        out_specs=[out_spec] * 6,
        out_shape=[out_shape] * 6,
        scratch_shapes=[pltpu.VMEM((tm, d), BF16)],
        compiler_params=_params("parallel"),
        name="inproj",
    )(x, mod, mod, nw, w, b, lbl)


def _inproj_conv(x, mod, nw, w, b, lbl, dw_w, dw_b, *, layer, tm, tc, seq_len, conv_tb):
    m, d = x.shape
    groups = d // LANES
    tiles = m // tm
    tile_of = lambda i: jnp.minimum(i, tiles - 1)
    conv_tile_of = lambda i: jnp.maximum(i - 1, 0)
    mod_spec = lambda row: _mod_spec(mod, row, d, tm, seq_len, tile_of)
    out_spec = pl.BlockSpec((tm * groups, LANES), lambda i: (tile_of(i), 0))
    conv_spec = pl.BlockSpec((tm * groups, LANES), lambda i: (conv_tile_of(i), 0))
    buf_spec = pl.BlockSpec((None, CONV_HIST, d), lambda i: (conv_tile_of(i) // (seq_len // tm), 0, 0))
    out_shape = jax.ShapeDtypeStruct((m * groups, LANES), F32)
    return pl.pallas_call(
        functools.partial(_inproj_conv_kernel, layer=layer, tc=tc, tiles_per_seq=seq_len // tm,
                          conv_tb=conv_tb),
        grid=(tiles + 1,),
        in_specs=[pl.BlockSpec((tm, d), lambda i: (tile_of(i), 0)), mod_spec(0), mod_spec(1),
                  _const_spec((1, d)), _const_spec(w.shape), _const_spec(b.shape),
                  _const_spec(lbl.shape), _const_spec((CONV_WIDTH, groups, LANES)),
                  _const_spec((1, groups, LANES))],
        out_specs=[out_spec, out_spec, out_spec, out_spec, conv_spec, out_spec, buf_spec],
        out_shape=[out_shape] * 6 + [jax.ShapeDtypeStruct((m // seq_len, CONV_HIST, d), F32)],
        scratch_shapes=[pltpu.VMEM((tm, d), BF16),
                        pltpu.VMEM(((tm + CONV_HIST) * groups, LANES), F32),
                        pltpu.VMEM((tm * groups, LANES), F32)],
        compiler_params=_params("arbitrary"),
        name="inproj_conv",
    )(x, mod, mod, nw, w, b, lbl, dw_w.reshape(CONV_WIDTH, groups, LANES), dw_b.reshape(1, groups, LANES))


def _split_level_table(c):
    t = np.arange(c)[:, None]
    s = np.arange(c)[None, :]
    x = t ^ s
    lvl = np.zeros((c, c), np.int32)
    for l in range(1, c.bit_length()):
        lvl[(x >> (l - 1)) == 1] = l
    return np.where(t > s, lvl, 0).astype(np.int32)


def _recur_kernel(q_ref, f_ref, v_ref, ga_ref, nw_ref, lvl_ref, o_ref, s_out_ref,
                  st_scr, x_scr, d_scr, *, heads, chunk):
    c_idx = pl.program_id(1)
    nlev = chunk.bit_length() - 1

    @pl.when(c_idx == 0)
    def _():
        st_scr[...] = jnp.zeros_like(st_scr)

    def to3(a):
        return a.reshape(chunk, heads, HEAD_DIM)

    def flat(a):
        return a.reshape(chunk * heads, HEAD_DIM)

    f3 = to3(f_ref[...])
    q3 = to3(q_ref[...])
    v3 = to3(v_ref[...])
    k3 = 1.0 - f3
    p, r = f3, jnp.ones_like(f3)
    half = 1
    for l in range(nlev):
        blk = 2 * half
        shp = (chunk // blk, blk, heads, HEAD_DIM)
        p4, r4, q4, k4 = (a.reshape(shp) for a in (p, r, q3, k3))
        x = jnp.concatenate([k4[:, :half] * r4[:, :half], q4[:, half:] * p4[:, half:]], axis=1)
        x_scr[l] = flat(x)
        left_tot = p4[:, half - 1:half]
        right_tot = p4[:, blk - 1:blk]
        p = jnp.concatenate([p4[:, :half], p4[:, half:] * left_tot], axis=1).reshape(f3.shape)
        r = jnp.concatenate([r4[:, :half] * right_tot, r4[:, half:]], axis=1).reshape(f3.shape)
        half = blk
    x_scr[nlev] = flat(q3 * p)
    x_scr[nlev + 1] = flat(k3 * r)
    d_scr[...] = flat(jnp.sum(q3 * k3, axis=-1, keepdims=True) * v3)
    p_last = p[chunk - 1]

    lvl = lvl_ref[...]
    for h in range(heads):
        rows = _group_rows(h, chunk, heads)
        att = jnp.zeros((chunk, chunk), F32)
        for l in range(nlev):
            xb = x_scr[l, rows, :].astype(BF16)
            a = lax.dot_general(xb, xb, NT, preferred_element_type=F32)
            att = jnp.where(lvl == l + 1, a, att)
        qt = x_scr[nlev, rows, :].astype(BF16)
        kt = x_scr[nlev + 1, rows, :].astype(BF16)
        vb = v_ref[rows, :].astype(BF16)
        st0 = st_scr[h]
        o = lax.dot_general(qt, st0.astype(BF16), NT, preferred_element_type=F32)
        o = o + jnp.dot(att.astype(BF16), vb, preferred_element_type=F32)
        o_ref[rows, :] = o + d_scr[rows, :]
        st_scr[h] = st0 * p_last[h:h + 1, :] + lax.dot_general(vb, kt, TN, preferred_element_type=F32)

    o3 = to3(o_ref[...])
    o3 = o3 * lax.rsqrt(jnp.mean(o3 * o3, axis=-1, keepdims=True) + EPS)
    o_ref[...] = flat(o3 * nw_ref[...] * to3(ga_ref[...]))

    @pl.when(c_idx == pl.num_programs(1) - 1)
    def _():
        for h in range(heads):
            s_out_ref[h] = st_scr[h].T


def _recur(q, f, v, ga, norm_w, *, n, heads, chunk):
    rows_total = q.shape[0]
    l = rows_total // (n * heads)
    nlev = chunk.bit_length() - 1
    rows = chunk * heads
    steps = l // chunk
    blk = pl.BlockSpec((rows, HEAD_DIM), lambda i, c: (i * steps + c, 0))
    st_spec = pl.BlockSpec((None, heads, HEAD_DIM, HEAD_DIM), lambda i, c: (i, 0, 0, 0))
    lvl = jnp.asarray(_split_level_table(chunk))
    return pl.pallas_call(
        functools.partial(_recur_kernel, heads=heads, chunk=chunk),
        grid=(n, steps),
        in_specs=[blk, blk, blk, blk, _const_spec((heads, HEAD_DIM)), _const_spec((chunk, chunk))],
        out_specs=[blk, st_spec],
        out_shape=[jax.ShapeDtypeStruct((rows_total, HEAD_DIM), F32),
                   jax.ShapeDtypeStruct((n, heads, HEAD_DIM, HEAD_DIM), F32)],
        scratch_shapes=[pltpu.VMEM((heads, HEAD_DIM, HEAD_DIM), F32),
                        pltpu.VMEM((nlev + 2, rows, HEAD_DIM), F32),
                        pltpu.VMEM((rows, HEAD_DIM), F32)],
        compiler_params=_params("parallel", "arbitrary"),
        name="recur",
    )(q, f, v, ga, norm_w.reshape(heads, HEAD_DIM), lvl)


def _recur_step_kernel(q_ref, f_ref, v_ref, ga_ref, nw_ref, s0_ref, o_ref, s_out_ref,
                       qt_scr, kt_scr, vp_scr, oi_scr, pt_scr, *, heads, steps, seqs, pad):
    rows_seq = steps * heads
    zero = jnp.zeros((pad * heads, HEAD_DIM), F32)
    qt_scr[...] = zero
    kt_scr[...] = zero
    vp_scr[...] = zero
    pt_scr[...] = jnp.zeros_like(pt_scr)

    def seq_body(g, carry):
        base = pl.multiple_of(g * rows_seq, rows_seq)
        blk = pl.ds(base, rows_seq)
        to3 = lambda a: a.reshape(steps, heads, HEAD_DIM)
        q3, f3, v3 = to3(q_ref[blk, :]), to3(f_ref[blk, :]), to3(v_ref[blk, :])
        k3 = 1.0 - f3
        p = [f3[0]]
        for t in range(1, steps):
            p.append(p[-1] * f3[t])
        r = [None] * steps
        r[steps - 1] = jnp.ones_like(f3[0])
        for s in range(steps - 2, -1, -1):
            r[s] = r[s + 1] * f3[s + 1]
        intra = []
        for t in range(steps):
            acc = jnp.sum(q3[t] * k3[t], axis=-1, keepdims=True) * v3[t]
            dec = None
            for s in range(t - 1, -1, -1):
                dec = f3[s + 1] if dec is None else dec * f3[s + 1]
                acc = acc + jnp.sum(q3[t] * k3[s] * dec, axis=-1, keepdims=True) * v3[s]
            intra.append(acc)
        for t in range(steps):
            rows_t = pl.ds(t * heads, heads)
            qt_scr[rows_t, :] = q3[t] * p[t]
            kt_scr[rows_t, :] = k3[t] * r[t]
            vp_scr[rows_t, :] = v3[t]
        pt_scr[0:heads, :] = p[steps - 1]
        p_col = pt_scr[...].T
        for h in range(heads):
            rows = _group_rows(h, pad, heads)
            qt = qt_scr[rows, :].astype(BF16)
            kt = kt_scr[rows, :].astype(BF16)
            vb = vp_scr[rows, :].astype(BF16)
            s0 = s0_ref[g, h]
            oi_scr[rows, :] = jnp.dot(qt, s0.astype(BF16), preferred_element_type=F32)
            s_out_ref[g, h] = s0 * p_col[:, h:h + 1] + lax.dot_general(
                kt, vb, TN, preferred_element_type=F32)
        o3 = to3(oi_scr[0:rows_seq, :]) + jnp.stack(intra, axis=0)
        o3 = o3 * lax.rsqrt(jnp.mean(o3 * o3, axis=-1, keepdims=True) + EPS)
        o_ref[blk, :] = (o3 * nw_ref[...] * to3(ga_ref[blk, :])).reshape(rows_seq, HEAD_DIM)
        return carry

    lax.fori_loop(0, seqs, seq_body, 0)


def _recur_step(q, f, v, ga, norm_w, s0, *, heads, steps, seqs, pad):
    n = s0.shape[0]
    rows = seqs * steps * heads
    blk = pl.BlockSpec((rows, HEAD_DIM), lambda i: (i, 0))
    st_spec = pl.BlockSpec((seqs, heads, HEAD_DIM, HEAD_DIM), lambda i: (i, 0, 0, 0))
    scr = pltpu.VMEM((pad * heads, HEAD_DIM), F32)
    return pl.pallas_call(
        functools.partial(_recur_step_kernel, heads=heads, steps=steps, seqs=seqs, pad=pad),
        grid=(n // seqs,),
        in_specs=[blk, blk, blk, blk, _const_spec((heads, HEAD_DIM)), st_spec],
        out_specs=[blk, st_spec],
        out_shape=[jax.ShapeDtypeStruct(q.shape, F32), jax.ShapeDtypeStruct(s0.shape, F32)],
        scratch_shapes=[scr, scr, scr, scr, pltpu.VMEM((HEAD_DIM, HEAD_DIM), F32)],
        compiler_params=_params("parallel"),
        name="recur_step",
    )(q, f, v, ga, norm_w.reshape(heads, HEAD_DIM), s0)


def _conv_taps(ext_scr, w_ref, b_ref, y_ref, *, ext_row0, out_row0, tb, groups):
    acc = [jnp.broadcast_to(b_ref[...], (tb, groups, LANES)), jnp.zeros((tb, groups, LANES), F32)]
    for j in range(CONV_WIDTH):
        start = (ext_row0 + j) * groups
        if not isinstance(start, int):
            start = pl.multiple_of(start, groups)
        e = ext_scr[pl.ds(start, tb * groups), :].reshape(tb, groups, LANES)
        acc[j % 2] = acc[j % 2] + w_ref[j] * e
    y = (acc[0] + acc[1]).reshape(tb * groups, LANES)
    y_ref[pl.ds(out_row0 * groups, tb * groups), :] = y
    return y


def _conv_step_kernel(u_ref, w_ref, b_ref, buf_ref, y_ref, buf_out_ref, ext_scr, *, steps, seqs, groups):
    hist = CONV_HIST * groups
    rows_seq = steps * groups
    for s in range(seqs):
        for g in range(groups):
            ext_scr[pl.ds(g, CONV_HIST, stride=groups), :] = buf_ref[s, :, g * LANES:(g + 1) * LANES]
        ext_scr[hist:hist + rows_seq, :] = u_ref[s * rows_seq:(s + 1) * rows_seq, :]
        _conv_taps(ext_scr, w_ref, b_ref, y_ref, ext_row0=0, out_row0=s * steps, tb=steps, groups=groups)
        for g in range(groups):
            buf_out_ref[s, :, g * LANES:(g + 1) * LANES] = ext_scr[
                pl.ds(rows_seq + g, CONV_HIST, stride=groups), :]


def _conv_step(u, dw_w, dw_b, buf, *, steps, seqs):
    n, _, ch = buf.shape
    groups = ch // LANES
    blk = pl.BlockSpec((seqs * steps * groups, LANES), lambda i: (i, 0))
    buf_spec = pl.BlockSpec((seqs, CONV_HIST, ch), lambda i: (i, 0, 0))
    return pl.pallas_call(
        functools.partial(_conv_step_kernel, steps=steps, seqs=seqs, groups=groups),
        grid=(n // seqs,),
        in_specs=[blk, _const_spec((CONV_WIDTH, groups, LANES)), _const_spec((1, groups, LANES)), buf_spec],
        out_specs=[blk, buf_spec],
        out_shape=[jax.ShapeDtypeStruct(u.shape, F32), jax.ShapeDtypeStruct(buf.shape, F32)],
        scratch_shapes=[pltpu.VMEM(((steps + CONV_HIST) * groups, LANES), F32)],
        compiler_params=_params("parallel"),
        name="conv_step",
    )(u, dw_w.reshape(CONV_WIDTH, groups, LANES), dw_b.reshape(1, groups, LANES), buf)


def _post_kernel(x_ref, oa_ref, yc_ref, sgb_ref, g1_ref, sh2_ref, sc2_ref, g2_ref, lnw_ref, lnb_ref,
                 wo_ref, nfw_ref, w1_ref, w2_ref, nlw_ref, y_ref, m_scr, c_scr, h_scr, acc_scr, *, fc):
    tm, d = x_ref.shape
    groups = d // LANES
    for g in range(groups):
        rows = _group_rows(g, tm, groups)
        cols = slice(g * LANES, (g + 1) * LANES)
        m_scr[:, cols] = oa_ref[rows, :].astype(BF16)
        c_scr[:, cols] = yc_ref[rows, :]
    yc = c_scr[...]
    mu = jnp.mean(yc, axis=-1, keepdims=True)
    dv = yc - mu
    var = jnp.mean(dv * dv, axis=-1, keepdims=True)
    ob = _silu(dv * lax.rsqrt(var + EPS) * lnw_ref[...] + lnb_ref[...])
    for g in range(groups):
        cols = slice(g * LANES, (g + 1) * LANES)
        m_scr[:, d + g * LANES:d + (g + 1) * LANES] = (
            ob[:, cols] * sgb_ref[_group_rows(g, tm, groups), :]).astype(BF16)
    merged = jnp.dot(m_scr[...], wo_ref[...], preferred_element_type=F32)
    x1 = x_ref[...] + g1_ref[...] * merged
    ms = jnp.mean(x1 * x1, axis=-1, keepdims=True)
    h2 = x1 * lax.rsqrt(ms + EPS) * nfw_ref[...]
    h2 = h2 * (1.0 + sc2_ref[...]) + sh2_ref[...]
    h_scr[...] = h2.astype(BF16)
    d_ff = w2_ref.shape[0]
    for c in range(d_ff // fc):
        gate = jnp.dot(h_scr[...], w1_ref[:, c * fc:(c + 1) * fc], preferred_element_type=F32)
        up = jnp.dot(h_scr[...], w1_ref[:, d_ff + c * fc:d_ff + (c + 1) * fc], preferred_element_type=F32)
        part = jnp.dot((_silu(gate) * up).astype(BF16), w2_ref[c * fc:(c + 1) * fc, :],
                       preferred_element_type=F32)
        if c == 0:
            acc_scr[...] = part
        else:
            acc_scr[...] += part
    x2 = x1 + g2_ref[...] * acc_scr[...]
    ms2 = jnp.mean(x2 * x2, axis=-1, keepdims=True)
    y_ref[...] = x2 * lax.rsqrt(ms2 + EPS) * nlw_ref[...]


def _post(x, oa, yc, sgb, mod, lnw, lnb, wo, nfw, w1, w2, nlw, *, tm, fc, rows_per_mod):
    m, d = x.shape
    groups = d // LANES
    mod_spec = lambda row: _mod_spec(mod, row, d, tm, rows_per_mod)
    row_spec = pl.BlockSpec((tm, d), lambda i: (i, 0))
    il_spec = pl.BlockSpec((tm * groups, LANES), lambda i: (i, 0))
    vec = _const_spec((1, d))
    return pl.pallas_call(
        functools.partial(_post_kernel, fc=fc),
        grid=(m // tm,),
        in_specs=[row_spec, il_spec, il_spec, il_spec, mod_spec(2), mod_spec(3), mod_spec(4), mod_spec(5),
                  vec, vec, _const_spec(wo.shape), vec, _const_spec(w1.shape), _const_spec(w2.shape), vec],
        out_specs=row_spec,
        out_shape=jax.ShapeDtypeStruct((m, d), F32),
        scratch_shapes=[pltpu.VMEM((tm, 2 * d), BF16), pltpu.VMEM((tm, d), F32),
                        pltpu.VMEM((tm, d), BF16), pltpu.VMEM((tm, d), F32)],
        compiler_params=_params("parallel"),
        name="post",
    )(x, oa, yc, sgb, mod, mod, mod, mod, lnw, lnb, wo, nfw, w1, w2, nlw)


INPROJ_TC = 256
FFN_CHUNK = 256
PROMPT_TM = 512
PROMPT_CHUNK = 128
PROMPT_CONV_TB = 8
SAMPLE_SEQS = 8
SAMPLE_PAD = 16


def kernel(x_prompt, x_sample, state_hgrn, state_conv, c_prompt, c_sample, w_ada, b_ada, norm_mix_w, w_in, b_in, lb_logits, hgrn_norm_w, conv_dw_w, conv_dw_b, conv_ln_w, conv_ln_b, w_out, norm_ffn_w, w_ffn_in, w_ffn_out, norm_final_w):
    depth = w_in.shape[0]
    assert depth == 1, "single-layer configuration"
    layer = 0
    n_p, l_p, d = x_prompt.shape
    n_s, l_s, _ = x_sample.shape
    heads = d // HEAD_DIM
    d_ff = w_ffn_out.shape[1]

    w_in_b = w_in[layer].astype(BF16)
    w1 = w_ffn_in[layer].astype(BF16)
    w2 = w_ffn_out[layer].astype(BF16)
    wo = w_out[layer].astype(BF16)
    row = lambda a: a.reshape(1, -1)

    mod = _mod(jnp.concatenate([c_prompt, c_sample], axis=0), w_ada[layer], b_ada[layer])
    mod_p = mod[:n_p].reshape(n_p, 1, MOD_ROWS * d)
    mod_s = jnp.repeat(mod[n_p:], l_s, axis=0).reshape(1, n_s * l_s, MOD_ROWS * d)

    def inproj(x2, mod_g, tm):
        return _inproj(x2, mod_g, row(norm_mix_w[layer]), w_in_b, row(b_in[layer]), lb_logits,
                       layer=layer, tm=tm, tc=INPROJ_TC, rows_per_mod=tm if mod_g.shape[1] > 1 else l_p)

    def post(x2, oa, yc, sgb, mod_g, tm):
        return _post(x2, oa, yc, sgb, mod_g, row(conv_ln_w[layer]), row(conv_ln_b[layer]), wo,
                     row(norm_ffn_w[layer]), w1, w2, row(norm_final_w),
                     tm=tm, fc=FFN_CHUNK, rows_per_mod=tm if mod_g.shape[1] > 1 else l_p)

    xp = x_prompt.reshape(n_p * l_p, d)
    q, f, v, ga, yc, sgb, b_p = _inproj_conv(
        xp, mod_p, row(norm_mix_w[layer]), w_in_b, row(b_in[layer]), lb_logits, conv_dw_w[layer],
        conv_dw_b[layer], layer=layer, tm=PROMPT_TM, tc=INPROJ_TC, seq_len=l_p, conv_tb=PROMPT_CONV_TB)
    oa, s_p = _recur(q, f, v, ga, hgrn_norm_w[layer], n=n_p, heads=heads, chunk=PROMPT_CHUNK)
    y_p = post(xp, oa, yc, sgb, mod_p, PROMPT_TM).reshape(n_p, l_p, d)

    xs = x_sample.reshape(n_s * l_s, d)
    q, f, v, ga, u, sgb = inproj(xs, mod_s, n_s * l_s)
    oa, s_s = _recur_step(q, f, v, ga, hgrn_norm_w[layer], state_hgrn[layer],
                          heads=heads, steps=l_s, seqs=SAMPLE_SEQS, pad=SAMPLE_PAD)
    yc, b_s = _conv_step(u, conv_dw_w[layer], conv_dw_b[layer], state_conv[layer], steps=l_s, seqs=SAMPLE_SEQS)
    y_s = post(xs, oa, yc, sgb, mod_s, n_s * l_s).reshape(n_s, l_s, d)

    return (y_p, y_s, s_p[None], b_p[None], s_s[None], b_s[None])
```

```python
import functools

import numpy as np
import jax
import jax.numpy as jnp
from jax import lax
from jax.experimental import pallas as pl
from jax.experimental.pallas import tpu as pltpu

F32 = jnp.float32
BF16 = jnp.bfloat16

EPS = 1e-6
LANES = 128
SUBLANES = 8
HEAD_DIM = LANES
CONV_WIDTH = 31
CONV_HIST = CONV_WIDTH - 1
VMEM_LIMIT_BYTES = 56 * 1024 * 1024

IN_SEGMENTS = 8
MOD_ROWS = 6

NT = (((1,), (1,)), ((), ()))
TN = (((0,), (0,)), ((), ()))


def _silu(x):
    return x * jax.nn.sigmoid(x)


def _params(*sem):
    return pltpu.CompilerParams(dimension_semantics=sem, vmem_limit_bytes=VMEM_LIMIT_BYTES)


def _const_spec(shape):
    nd = len(shape)
    return pl.BlockSpec(shape, lambda *_: (0,) * nd, pipeline_mode=pl.Buffered(1))


def _group_rows(g, n, groups):
    return pl.ds(g, n, stride=groups)


def _mod_kernel(c_ref, w_ref, b_ref, o_ref):
    a = _silu(c_ref[...]).astype(BF16)
    o_ref[...] = jnp.dot(a, w_ref[...].astype(BF16), preferred_element_type=F32) + b_ref[...]


def _mod(c, w_ada, b_ada, *, tn=512):
    n, d = c.shape
    cols = w_ada.shape[1]
    return pl.pallas_call(
        _mod_kernel,
        grid=(cols // tn,),
        in_specs=[pl.BlockSpec((n, d), lambda j: (0, 0)),
                  pl.BlockSpec((d, tn), lambda j: (0, j)),
                  pl.BlockSpec((1, tn), lambda j: (0, j))],
        out_specs=pl.BlockSpec((n, tn), lambda j: (0, j)),
        out_shape=jax.ShapeDtypeStruct((n, cols), F32),
        compiler_params=_params("parallel"),
        name="mod",
    )(c, w_ada, b_ada.reshape(1, cols))


def _inproj_kernel(x_ref, sh_ref, sc_ref, nw_ref, w_ref, b_ref, lbl_ref,
                   q_ref, f_ref, v_ref, ga_ref, u_ref, sgb_ref, h_scr, *, layer, tc):
    tm, d = x_ref.shape
    groups = d // LANES
    x = x_ref[...]
    ms = jnp.mean(x * x, axis=-1, keepdims=True)
    h = x * lax.rsqrt(ms + EPS) * nw_ref[...]
    h = h * (1.0 + sc_ref[...]) + sh_ref[...]
    h_scr[...] = h.astype(BF16)
    for j in range(d // tc):
        def seg(s):
            cols = slice(s * d + j * tc, s * d + (j + 1) * tc)
            return jnp.dot(h_scr[...], w_ref[:, cols], preferred_element_type=F32) + b_ref[:, cols]

        def put(ref, val):
            for g in range(tc // LANES):
                rows = _group_rows(j * (tc // LANES) + g, tm, groups)
                ref[rows, :] = val[:, g * LANES:(g + 1) * LANES]

        put(q_ref, _silu(seg(0)))
        lg = lbl_ref[:, j * tc:(j + 1) * tc]
        e = jnp.exp(lg - jnp.max(lg, axis=0, keepdims=True))
        lb = jnp.sum(e[:layer + 1], axis=0, keepdims=True) / jnp.sum(e, axis=0, keepdims=True)
        put(f_ref, lb + (1.0 - lb) * jax.nn.sigmoid(seg(1)))
        put(v_ref, seg(2))
        og_act = _silu(seg(3))
        put(ga_ref, og_act * jax.nn.sigmoid(seg(6)))
        glu_a = seg(4)
        put(u_ref, glu_a * jax.nn.sigmoid(seg(5)))
        put(sgb_ref, jax.nn.sigmoid(seg(7)))


def _mod_spec(mod, row, d, tm, rows_per_mod):
    tiles_per_mod = rows_per_mod // tm
    return pl.BlockSpec((None, mod.shape[1], d), lambda i: (i // tiles_per_mod, 0, row))


def _inproj(x, mod, nw, w, b, lbl, *, layer, tm, tc, rows_per_mod):
    m, d = x.shape
    groups = d // LANES
    mod_spec = lambda row: _mod_spec(mod, row, d, tm, rows_per_mod)
    out_spec = pl.BlockSpec((tm * groups, LANES), lambda i: (i, 0))
    out_shape = jax.ShapeDtypeStruct((m * groups, LANES), F32)
    return pl.pallas_call(
        functools.partial(_inproj_kernel, layer=layer, tc=tc),
        grid=(m // tm,),
        in_specs=[pl.BlockSpec((tm, d), lambda i: (i, 0)), mod_spec(0), mod_spec(1),
                  _const_spec((1, d)), _const_spec(w.shape), _const_spec(b.shape),
                  _const_spec(lbl.shape)],
        out_specs=[out_spec] * 6,
        out_shape=[out_shape] * 6,
        scratch_shapes=[pltpu.VMEM((tm, d), BF16)],
        compiler_params=_params("parallel"),
        name="inproj",
    )(x, mod, mod, nw, w, b, lbl)


def _split_level_table(c):
    t = np.arange(c)[:, None]
    s = np.arange(c)[None, :]
    x = t ^ s
    lvl = np.zeros((c, c), np.int32)
    for l in range(1, c.bit_length()):
        lvl[(x >> (l - 1)) == 1] = l
    return np.where(t > s, lvl, 0).astype(np.int32)


def _recur_kernel(q_ref, f_ref, v_ref, ga_ref, nw_ref, lvl_ref, o_ref, s_out_ref,
                  st_scr, x_scr, d_scr, *, heads, chunk):
    c_idx = pl.program_id(1)
    nlev = chunk.bit_length() - 1

    @pl.when(c_idx == 0)
    def _():
        st_scr[...] = jnp.zeros_like(st_scr)

    def to3(a):
        return a.reshape(chunk, heads, HEAD_DIM)

    def flat(a):
        return a.reshape(chunk * heads, HEAD_DIM)

    f3 = to3(f_ref[...])
    q3 = to3(q_ref[...])
    v3 = to3(v_ref[...])
    k3 = 1.0 - f3
    p, r = f3, jnp.ones_like(f3)
    half = 1
    for l in range(nlev):
        blk = 2 * half
        shp = (chunk // blk, blk, heads, HEAD_DIM)
        p4, r4, q4, k4 = (a.reshape(shp) for a in (p, r, q3, k3))
        x = jnp.concatenate([k4[:, :half] * r4[:, :half], q4[:, half:] * p4[:, half:]], axis=1)
        x_scr[l] = flat(x)
        left_tot = p4[:, half - 1:half]
        right_tot = p4[:, blk - 1:blk]
        p = jnp.concatenate([p4[:, :half], p4[:, half:] * left_tot], axis=1).reshape(f3.shape)
        r = jnp.concatenate([r4[:, :half] * right_tot, r4[:, half:]], axis=1).reshape(f3.shape)
        half = blk
    x_scr[nlev] = flat(q3 * p)
    x_scr[nlev + 1] = flat(k3 * r)
    d_scr[...] = flat(jnp.sum(q3 * k3, axis=-1, keepdims=True) * v3)
    p_last = p[chunk - 1]

    lvl = lvl_ref[...]
    for h in range(heads):
        rows = _group_rows(h, chunk, heads)
        att = jnp.zeros((chunk, chunk), F32)
        for l in range(nlev):
            xb = x_scr[l, rows, :].astype(BF16)
            a = lax.dot_general(xb, xb, NT, preferred_element_type=F32)
            att = jnp.where(lvl == l + 1, a, att)
        qt = x_scr[nlev, rows, :].astype(BF16)
        kt = x_scr[nlev + 1, rows, :].astype(BF16)
        vb = v_ref[rows, :].astype(BF16)
        st0 = st_scr[h]
        o = lax.dot_general(qt, st0.astype(BF16), NT, preferred_element_type=F32)
        o = o + jnp.dot(att.astype(BF16), vb, preferred_element_type=F32)
        o_ref[rows, :] = o + d_scr[rows, :]
        st_scr[h] = st0 * p_last[h:h + 1, :] + lax.dot_general(vb, kt, TN, preferred_element_type=F32)

    o3 = to3(o_ref[...])
    o3 = o3 * lax.rsqrt(jnp.mean(o3 * o3, axis=-1, keepdims=True) + EPS)
    o_ref[...] = flat(o3 * nw_ref[...] * to3(ga_ref[...]))

    @pl.when(c_idx == pl.num_programs(1) - 1)
    def _():
        for h in range(heads):
            s_out_ref[h] = st_scr[h].T


def _recur(q, f, v, ga, norm_w, *, n, heads, chunk):
    rows_total = q.shape[0]
    l = rows_total // (n * heads)
    nlev = chunk.bit_length() - 1
    rows = chunk * heads
    steps = l // chunk
    blk = pl.BlockSpec((rows, HEAD_DIM), lambda i, c: (i * steps + c, 0))
    st_spec = pl.BlockSpec((None, None, heads, HEAD_DIM, HEAD_DIM), lambda i, c: (0, i, 0, 0, 0))
    lvl = jnp.asarray(_split_level_table(chunk))
    return pl.pallas_call(
        functools.partial(_recur_kernel, heads=heads, chunk=chunk),
        grid=(n, steps),
        in_specs=[blk, blk, blk, blk, _const_spec((heads, HEAD_DIM)), _const_spec((chunk, chunk))],
        out_specs=[blk, st_spec],
        out_shape=[jax.ShapeDtypeStruct((rows_total, HEAD_DIM), F32),
                   jax.ShapeDtypeStruct((1, n, heads, HEAD_DIM, HEAD_DIM), F32)],
        scratch_shapes=[pltpu.VMEM((heads, HEAD_DIM, HEAD_DIM), F32),
                        pltpu.VMEM((nlev + 2, rows, HEAD_DIM), F32),
                        pltpu.VMEM((rows, HEAD_DIM), F32)],
        compiler_params=_params("parallel", "arbitrary"),
        name="recur",
    )(q, f, v, ga, norm_w.reshape(heads, HEAD_DIM), lvl)


def _recur_step_kernel(q_ref, f_ref, v_ref, ga_ref, nw_ref, s0_ref, o_ref, s_out_ref,
                       qt_scr, kt_scr, vp_scr, oi_scr, pt_scr, *, heads, steps, seqs, pad):
    rows_seq = steps * heads
    zero = jnp.zeros((pad * heads, HEAD_DIM), F32)
    qt_scr[...] = zero
    kt_scr[...] = zero
    vp_scr[...] = zero
    pt_scr[...] = jnp.zeros_like(pt_scr)

    def seq_body(g, carry):
        base = pl.multiple_of(g * rows_seq, rows_seq)
        blk = pl.ds(base, rows_seq)
        to3 = lambda a: a.reshape(steps, heads, HEAD_DIM)
        q3, f3, v3 = to3(q_ref[blk, :]), to3(f_ref[blk, :]), to3(v_ref[blk, :])
        k3 = 1.0 - f3
        p = [f3[0]]
        for t in range(1, steps):
            p.append(p[-1] * f3[t])
        r = [None] * steps
        r[steps - 1] = jnp.ones_like(f3[0])
        for s in range(steps - 2, -1, -1):
            r[s] = r[s + 1] * f3[s + 1]
        intra = []
        for t in range(steps):
            acc = jnp.sum(q3[t] * k3[t], axis=-1, keepdims=True) * v3[t]
            dec = None
            for s in range(t - 1, -1, -1):
                dec = f3[s + 1] if dec is None else dec * f3[s + 1]
                acc = acc + jnp.sum(q3[t] * k3[s] * dec, axis=-1, keepdims=True) * v3[s]
            intra.append(acc)
        for t in range(steps):
            rows_t = pl.ds(t * heads, heads)
            qt_scr[rows_t, :] = q3[t] * p[t]
            kt_scr[rows_t, :] = k3[t] * r[t]
            vp_scr[rows_t, :] = v3[t]
        pt_scr[0:heads, :] = p[steps - 1]
        p_col = pt_scr[...].T
        for h in range(heads):
            rows = _group_rows(h, pad, heads)
            qt = qt_scr[rows, :].astype(BF16)
            kt = kt_scr[rows, :].astype(BF16)
            vb = vp_scr[rows, :].astype(BF16)
            s0 = s0_ref[g, h]
            oi_scr[rows, :] = jnp.dot(qt, s0.astype(BF16), preferred_element_type=F32)
            s_out_ref[g, h] = s0 * p_col[:, h:h + 1] + lax.dot_general(
                kt, vb, TN, preferred_element_type=F32)
        o3 = to3(oi_scr[0:rows_seq, :]) + jnp.stack(intra, axis=0)
        o3 = o3 * lax.rsqrt(jnp.mean(o3 * o3, axis=-1, keepdims=True) + EPS)
        o_ref[blk, :] = (o3 * nw_ref[...] * to3(ga_ref[blk, :])).reshape(rows_seq, HEAD_DIM)
        return carry

    lax.fori_loop(0, seqs, seq_body, 0)


def _recur_step(q, f, v, ga, norm_w, states, *, layer, heads, steps, seqs, pad):
    n = states.shape[1]
    rows = seqs * steps * heads
    blk = pl.BlockSpec((rows, HEAD_DIM), lambda i: (i, 0))
    st_blk = (None, seqs, heads, HEAD_DIM, HEAD_DIM)
    scr = pltpu.VMEM((pad * heads, HEAD_DIM), F32)
    return pl.pallas_call(
        functools.partial(_recur_step_kernel, heads=heads, steps=steps, seqs=seqs, pad=pad),
        grid=(n // seqs,),
        in_specs=[blk, blk, blk, blk, _const_spec((heads, HEAD_DIM)),
                  pl.BlockSpec(st_blk, lambda i: (layer, i, 0, 0, 0))],
        out_specs=[blk, pl.BlockSpec(st_blk, lambda i: (0, i, 0, 0, 0))],
        out_shape=[jax.ShapeDtypeStruct(q.shape, F32), jax.ShapeDtypeStruct((1,) + states.shape[1:], F32)],
        scratch_shapes=[scr, scr, scr, scr, pltpu.VMEM((HEAD_DIM, HEAD_DIM), F32)],
        compiler_params=_params("parallel"),
        name="recur_step",
    )(q, f, v, ga, norm_w.reshape(heads, HEAD_DIM), states)


def _conv_taps(ext_scr, w_ref, b_ref, y_ref, *, ext_row0, out_row0, tb, groups):
    acc = [jnp.broadcast_to(b_ref[...], (tb, groups, LANES)), jnp.zeros((tb, groups, LANES), F32)]
    for j in range(CONV_WIDTH):
        start = (ext_row0 + j) * groups
        if not isinstance(start, int):
            start = pl.multiple_of(start, groups)
        e = ext_scr[pl.ds(start, tb * groups), :].reshape(tb, groups, LANES)
        acc[j % 2] = acc[j % 2] + w_ref[j] * e
    y_ref[pl.ds(out_row0 * groups, tb * groups), :] = (acc[0] + acc[1]).reshape(tb * groups, LANES)


def _conv_kernel(u_ref, w_ref, b_ref, y_ref, buf_out_ref, ext_scr, *, tile, tb, groups):
    t_idx = pl.program_id(1)
    hist = CONV_HIST * groups

    @pl.when(t_idx == 0)
    def _():
        ext_scr[0:hist, :] = jnp.zeros((hist, LANES), F32)

    @pl.when(t_idx > 0)
    def _():
        ext_scr[0:hist, :] = ext_scr[tile * groups:tile * groups + hist, :]

    ext_scr[hist:hist + tile * groups, :] = u_ref[...]

    def block_body(i, carry):
        r0 = pl.multiple_of(i * tb, tb)
        _conv_taps(ext_scr, w_ref, b_ref, y_ref, ext_row0=r0, out_row0=r0, tb=tb, groups=groups)
        return carry

    lax.fori_loop(0, tile // tb, block_body, 0)

    @pl.when(t_idx == pl.num_programs(1) - 1)
    def _():
        for g in range(groups):
            buf_out_ref[:, g * LANES:(g + 1) * LANES] = ext_scr[
                pl.ds(tile * groups + g, CONV_HIST, stride=groups), :]


def _conv(u, dw_w, dw_b, *, n, tile, tb):
    ch = dw_w.shape[1]
    groups = ch // LANES
    l = u.shape[0] // (n * groups)
    steps = l // tile
    blk = pl.BlockSpec((tile * groups, LANES), lambda i, t: (i * steps + t, 0))
    return pl.pallas_call(
        functools.partial(_conv_kernel, tile=tile, tb=tb, groups=groups),
        grid=(n, steps),
        in_specs=[blk, _const_spec((CONV_WIDTH, groups, LANES)), _const_spec((1, groups, LANES))],
        out_specs=[blk, pl.BlockSpec((None, None, CONV_HIST, ch), lambda i, t: (0, i, 0, 0))],
        out_shape=[jax.ShapeDtypeStruct(u.shape, F32), jax.ShapeDtypeStruct((1, n, CONV_HIST, ch), F32)],
        scratch_shapes=[pltpu.VMEM(((tile + CONV_HIST) * groups, LANES), F32)],
        compiler_params=_params("parallel", "arbitrary"),
        name="conv",
    )(u, dw_w.reshape(CONV_WIDTH, groups, LANES), dw_b.reshape(1, groups, LANES))


def _conv_step_kernel(u_ref, w_ref, b_ref, buf_ref, y_ref, buf_out_ref, ext_scr, *, steps, seqs, groups):
    hist = CONV_HIST * groups
    rows_seq = steps * groups
    for s in range(seqs):
        for g in range(groups):
            ext_scr[pl.ds(g, CONV_HIST, stride=groups), :] = buf_ref[s, :, g * LANES:(g + 1) * LANES]
        ext_scr[hist:hist + rows_seq, :] = u_ref[s * rows_seq:(s + 1) * rows_seq, :]
        _conv_taps(ext_scr, w_ref, b_ref, y_ref, ext_row0=0, out_row0=s * steps, tb=steps, groups=groups)
        for g in range(groups):
            buf_out_ref[s, :, g * LANES:(g + 1) * LANES] = ext_scr[
                pl.ds(rows_seq + g, CONV_HIST, stride=groups), :]


def _conv_step(u, dw_w, dw_b, bufs, *, layer, steps, seqs):
    _, n, _, ch = bufs.shape
    groups = ch // LANES
    blk = pl.BlockSpec((seqs * steps * groups, LANES), lambda i: (i, 0))
    buf_blk = (None, seqs, CONV_HIST, ch)
    return pl.pallas_call(
        functools.partial(_conv_step_kernel, steps=steps, seqs=seqs, groups=groups),
        grid=(n // seqs,),
        in_specs=[blk, _const_spec((CONV_WIDTH, groups, LANES)), _const_spec((1, groups, LANES)),
                  pl.BlockSpec(buf_blk, lambda i: (layer, i, 0, 0))],
        out_specs=[blk, pl.BlockSpec(buf_blk, lambda i: (0, i, 0, 0))],
        out_shape=[jax.ShapeDtypeStruct(u.shape, F32), jax.ShapeDtypeStruct((1,) + bufs.shape[1:], F32)],
        scratch_shapes=[pltpu.VMEM(((steps + CONV_HIST) * groups, LANES), F32)],
        compiler_params=_params("parallel"),
        name="conv_step",
    )(u, dw_w.reshape(CONV_WIDTH, groups, LANES), dw_b.reshape(1, groups, LANES), bufs)


def _post_kernel(x_ref, oa_ref, yc_ref, sgb_ref, g1_ref, sh2_ref, sc2_ref, g2_ref, lnw_ref, lnb_ref,
                 wo_ref, nfw_ref, w1_ref, w2_ref, nlw_ref, y_ref, m_scr, c_scr, h_scr, acc_scr, *, fc):
    tm, d = x_ref.shape
    groups = d // LANES
    for g in range(groups):
        rows = _group_rows(g, tm, groups)
        cols = slice(g * LANES, (g + 1) * LANES)
        m_scr[:, cols] = oa_ref[rows, :].astype(BF16)
        c_scr[:, cols] = yc_ref[rows, :]
    yc = c_scr[...]
    mu = jnp.mean(yc, axis=-1, keepdims=True)
    dv = yc - mu
    var = jnp.mean(dv * dv, axis=-1, keepdims=True)
    ob = _silu(dv * lax.rsqrt(var + EPS) * lnw_ref[...] + lnb_ref[...])
    for g in range(groups):
        cols = slice(g * LANES, (g + 1) * LANES)
        m_scr[:, d + g * LANES:d + (g + 1) * LANES] = (
            ob[:, cols] * sgb_ref[_group_rows(g, tm, groups), :]).astype(BF16)
    merged = jnp.dot(m_scr[...], wo_ref[...], preferred_element_type=F32)
    x1 = x_ref[...] + g1_ref[...] * merged
    ms = jnp.mean(x1 * x1, axis=-1, keepdims=True)
    h2 = x1 * lax.rsqrt(ms + EPS) * nfw_ref[...]
    h2 = h2 * (1.0 + sc2_ref[...]) + sh2_ref[...]
    h_scr[...] = h2.astype(BF16)
    d_ff = w2_ref.shape[0]
    for c in range(d_ff // fc):
        gate = jnp.dot(h_scr[...], w1_ref[:, c * fc:(c + 1) * fc], preferred_element_type=F32)
        up = jnp.dot(h_scr[...], w1_ref[:, d_ff + c * fc:d_ff + (c + 1) * fc], preferred_element_type=F32)
        part = jnp.dot((_silu(gate) * up).astype(BF16), w2_ref[c * fc:(c + 1) * fc, :],
                       preferred_element_type=F32)
        if c == 0:
            acc_scr[...] = part
        else:
            acc_scr[...] += part
    x2 = x1 + g2_ref[...] * acc_scr[...]
    ms2 = jnp.mean(x2 * x2, axis=-1, keepdims=True)
    y_ref[...] = x2 * lax.rsqrt(ms2 + EPS) * nlw_ref[...]


def _post(x, oa, yc, sgb, mod, lnw, lnb, wo, nfw, w1, w2, nlw, *, tm, fc, rows_per_mod):
    m, d = x.shape
    groups = d // LANES
    mod_spec = lambda row: _mod_spec(mod, row, d, tm, rows_per_mod)
    row_spec = pl.BlockSpec((tm, d), lambda i: (i, 0))
    il_spec = pl.BlockSpec((tm * groups, LANES), lambda i: (i, 0))
    vec = _const_spec((1, d))
    return pl.pallas_call(
        functools.partial(_post_kernel, fc=fc),
        grid=(m // tm,),
        in_specs=[row_spec, il_spec, il_spec, il_spec, mod_spec(2), mod_spec(3), mod_spec(4), mod_spec(5),
                  vec, vec, _const_spec(wo.shape), vec, _const_spec(w1.shape), _const_spec(w2.shape), vec],
        out_specs=row_spec,
        out_shape=jax.ShapeDtypeStruct((m, d), F32),
        scratch_shapes=[pltpu.VMEM((tm, 2 * d), BF16), pltpu.VMEM((tm, d), F32),
                        pltpu.VMEM((tm, d), BF16), pltpu.VMEM((tm, d), F32)],
        compiler_params=_params("parallel"),
        name="post",
    )(x, oa, yc, sgb, mod, mod, mod, mod, lnw, lnb, wo, nfw, w1, w2, nlw)


INPROJ_TC = 256
FFN_CHUNK = 256
MOD_TN = 1024
PROMPT_TM = 512
PROMPT_CHUNK = 128
PROMPT_CONV_TILE = 256
PROMPT_CONV_TB = 16
SAMPLE_SEQS = 16
SAMPLE_PAD = 16


def kernel(x_prompt, x_sample, state_hgrn, state_conv, c_prompt, c_sample, w_ada, b_ada, norm_mix_w, w_in, b_in, lb_logits, hgrn_norm_w, conv_dw_w, conv_dw_b, conv_ln_w, conv_ln_b, w_out, norm_ffn_w, w_ffn_in, w_ffn_out, norm_final_w):
    depth = w_in.shape[0]
    assert depth == 1, "single-layer configuration"
    layer = 0
    n_p, l_p, d = x_prompt.shape
    n_s, l_s, _ = x_sample.shape
    heads = d // HEAD_DIM

    w_in_b = w_in[layer].astype(BF16)
    w1 = w_ffn_in[layer].astype(BF16)
    w2 = w_ffn_out[layer].astype(BF16)
    wo = w_out[layer].astype(BF16)
    row = lambda a: a.reshape(1, -1)

    mod = _mod(jnp.concatenate([c_prompt, c_sample], axis=0), w_ada[layer], b_ada[layer], tn=MOD_TN)
    mod_p = mod[:n_p].reshape(n_p, 1, MOD_ROWS * d)
    mod_s = jnp.repeat(mod[n_p:], l_s, axis=0).reshape(1, n_s * l_s, MOD_ROWS * d)

    def inproj(x2, mod_g, tm):
        return _inproj(x2, mod_g, row(norm_mix_w[layer]), w_in_b, row(b_in[layer]), lb_logits,
                       layer=layer, tm=tm, tc=INPROJ_TC, rows_per_mod=tm if mod_g.shape[1] > 1 else l_p)

    def post(x2, oa, yc, sgb, mod_g, tm):
        return _post(x2, oa, yc, sgb, mod_g, row(conv_ln_w[layer]), row(conv_ln_b[layer]), wo,
                     row(norm_ffn_w[layer]), w1, w2, row(norm_final_w),
                     tm=tm, fc=FFN_CHUNK, rows_per_mod=tm if mod_g.shape[1] > 1 else l_p)

    xp = x_prompt.reshape(n_p * l_p, d)
    q, f, v, ga, u, sgb = inproj(xp, mod_p, PROMPT_TM)
    oa, s_p = _recur(q, f, v, ga, hgrn_norm_w[layer], n=n_p, heads=heads, chunk=PROMPT_CHUNK)
    yc, b_p = _conv(u, conv_dw_w[layer], conv_dw_b[layer], n=n_p, tile=PROMPT_CONV_TILE, tb=PROMPT_CONV_TB)
    y_p = post(xp, oa, yc, sgb, mod_p, PROMPT_TM).reshape(n_p, l_p, d)

    xs = x_sample.reshape(n_s * l_s, d)
    q, f, v, ga, u, sgb = inproj(xs, mod_s, n_s * l_s)
    oa, s_s = _recur_step(q, f, v, ga, hgrn_norm_w[layer], state_hgrn, layer=layer,
                          heads=heads, steps=l_s, seqs=SAMPLE_SEQS, pad=SAMPLE_PAD)
    yc, b_s = _conv_step(u, conv_dw_w[layer], conv_dw_b[layer], state_conv, layer=layer,
                         steps=l_s, seqs=SAMPLE_SEQS)
    y_s = post(xs, oa, yc, sgb, mod_s, n_s * l_s).reshape(n_s, l_s, d)

    return (y_p, y_s, s_p, b_p, s_s, b_s)
```

```python
import functools

import numpy as np
import jax
import jax.numpy as jnp
from jax import lax
from jax.experimental import pallas as pl
from jax.experimental.pallas import tpu as pltpu

F32 = jnp.float32
BF16 = jnp.bfloat16

EPS = 1e-6
LANES = 128
SUBLANES = 8
HEAD_DIM = LANES
CONV_WIDTH = 31
CONV_HIST = CONV_WIDTH - 1
VMEM_LIMIT_BYTES = 56 * 1024 * 1024

IN_SEGMENTS = 8
MOD_ROWS = 6

NT = (((1,), (1,)), ((), ()))
TN = (((0,), (0,)), ((), ()))


def _silu(x):
    return x * jax.nn.sigmoid(x)


def _params(*sem):
    return pltpu.CompilerParams(dimension_semantics=sem, vmem_limit_bytes=VMEM_LIMIT_BYTES)


def _const_spec(shape):
    nd = len(shape)
    return pl.BlockSpec(shape, lambda *_: (0,) * nd, pipeline_mode=pl.Buffered(1))


def _group_rows(g, n, groups):
    return pl.ds(g, n, stride=groups)


def _mod_kernel(c_ref, w_ref, b_ref, o_ref):
    a = _silu(c_ref[...]).astype(BF16)
    o_ref[...] = jnp.dot(a, w_ref[...].astype(BF16), preferred_element_type=F32) + b_ref[...]


def _mod(c, w_ada, b_ada, *, tn=512):
    n, d = c.shape
    cols = w_ada.shape[1]
    return pl.pallas_call(
        _mod_kernel,
        grid=(cols // tn,),
        in_specs=[pl.BlockSpec((n, d), lambda j: (0, 0)),
                  pl.BlockSpec((d, tn), lambda j: (0, j)),
                  pl.BlockSpec((1, tn), lambda j: (0, j))],
        out_specs=pl.BlockSpec((n, tn), lambda j: (0, j)),
        out_shape=jax.ShapeDtypeStruct((n, cols), F32),
        compiler_params=_params("parallel"),
        name="mod",
    )(c, w_ada, b_ada.reshape(1, cols))


def _inproj_kernel(x_ref, sh_ref, sc_ref, nw_ref, w_ref, b_ref, lbl_ref,
                   q_ref, f_ref, v_ref, ga_ref, u_ref, sgb_ref, h_scr, *, layer, tc):
    tm, d = x_ref.shape
    groups = d // LANES
    x = x_ref[...]
    ms = jnp.mean(x * x, axis=-1, keepdims=True)
    h = x * lax.rsqrt(ms + EPS) * nw_ref[...]
    h = h * (1.0 + sc_ref[...]) + sh_ref[...]
    h_scr[...] = h.astype(BF16)
    for j in range(d // tc):
        def seg(s):
            cols = slice(s * d + j * tc, s * d + (j + 1) * tc)
            return jnp.dot(h_scr[...], w_ref[:, cols], preferred_element_type=F32) + b_ref[:, cols]

        def put(ref, val):
            for g in range(tc // LANES):
                rows = _group_rows(j * (tc // LANES) + g, tm, groups)
                ref[rows, :] = val[:, g * LANES:(g + 1) * LANES]

        put(q_ref, _silu(seg(0)))
        lg = lbl_ref[:, j * tc:(j + 1) * tc]
        e = jnp.exp(lg - jnp.max(lg, axis=0, keepdims=True))
        lb = jnp.sum(e[:layer + 1], axis=0, keepdims=True) / jnp.sum(e, axis=0, keepdims=True)
        put(f_ref, lb + (1.0 - lb) * jax.nn.sigmoid(seg(1)))
        put(v_ref, seg(2))
        og_act = _silu(seg(3))
        put(ga_ref, og_act * jax.nn.sigmoid(seg(6)))
        glu_a = seg(4)
        put(u_ref, glu_a * jax.nn.sigmoid(seg(5)))
        put(sgb_ref, jax.nn.sigmoid(seg(7)))


def _mod_spec(mod, row, d, tm, rows_per_mod):
    tiles_per_mod = rows_per_mod // tm
    return pl.BlockSpec((None, mod.shape[1], d), lambda i: (i // tiles_per_mod, 0, row))


def _inproj(x, mod, nw, w, b, lbl, *, layer, tm, tc, rows_per_mod):
    m, d = x.shape
    groups = d // LANES
    mod_spec = lambda row: _mod_spec(mod, row, d, tm, rows_per_mod)
    out_spec = pl.BlockSpec((tm * groups, LANES), lambda i: (i, 0))
    out_shape = jax.ShapeDtypeStruct((m * groups, LANES), F32)
    return pl.pallas_call(
        functools.partial(_inproj_kernel, layer=layer, tc=tc),
        grid=(m // tm,),
        in_specs=[pl.BlockSpec((tm, d), lambda i: (i, 0)), mod_spec(0), mod_spec(1),
                  _const_spec((1, d)), _const_spec(w.shape), _const_spec(b.shape),
                  _const_spec(lbl.shape)],
        out_specs=[out_spec] * 6,
        out_shape=[out_shape] * 6,
        scratch_shapes=[pltpu.VMEM((tm, d), BF16)],
        compiler_params=_params("parallel"),
        name="inproj",
    )(x, mod, mod, nw, w, b, lbl)


def _split_level_table(c):
    t = np.arange(c)[:, None]
    s = np.arange(c)[None, :]
    x = t ^ s
    lvl = np.zeros((c, c), np.int32)
    for l in range(1, c.bit_length()):
        lvl[(x >> (l - 1)) == 1] = l
    return np.where(t > s, lvl, 0).astype(np.int32)


def _recur_kernel(q_ref, f_ref, v_ref, ga_ref, nw_ref, lvl_ref, o_ref, s_out_ref,
                  st_scr, x_scr, d_scr, *, heads, chunk):
    c_idx = pl.program_id(1)
    nlev = chunk.bit_length() - 1

    @pl.when(c_idx == 0)
    def _():
        st_scr[...] = jnp.zeros_like(st_scr)

    def to3(a):
        return a.reshape(chunk, heads, HEAD_DIM)

    def flat(a):
        return a.reshape(chunk * heads, HEAD_DIM)

    lvl = lvl_ref[...]
    rows_chunk = chunk * heads

    def chunk_body(ci, carry):
        base = pl.multiple_of(ci * rows_chunk, rows_chunk)
        whole = pl.ds(base, rows_chunk)
        f3 = to3(f_ref[whole, :])
        q3 = to3(q_ref[whole, :])
        v3 = to3(v_ref[whole, :])
        k3 = 1.0 - f3
        p, r = f3, jnp.ones_like(f3)
        half = 1
        for l in range(nlev):
            blk = 2 * half
            shp = (chunk // blk, blk, heads, HEAD_DIM)
            p4, r4, q4, k4 = (a.reshape(shp) for a in (p, r, q3, k3))
            x = jnp.concatenate([k4[:, :half] * r4[:, :half], q4[:, half:] * p4[:, half:]], axis=1)
            x_scr[l] = flat(x)
            left_tot = p4[:, half - 1:half]
            right_tot = p4[:, blk - 1:blk]
            p = jnp.concatenate([p4[:, :half], p4[:, half:] * left_tot], axis=1).reshape(f3.shape)
            r = jnp.concatenate([r4[:, :half] * right_tot, r4[:, half:]], axis=1).reshape(f3.shape)
            half = blk
        x_scr[nlev] = flat(q3 * p)
        x_scr[nlev + 1] = flat(k3 * r)
        d_scr[...] = flat(jnp.sum(q3 * k3, axis=-1, keepdims=True) * v3)
        p_last = p[chunk - 1]

        for h in range(heads):
            rows = _group_rows(h, chunk, heads)
            blk_rows = _group_rows(base + h, chunk, heads)
            att = jnp.zeros((chunk, chunk), F32)
            for l in range(nlev):
                xb = x_scr[l, rows, :].astype(BF16)
                a = lax.dot_general(xb, xb, NT, preferred_element_type=F32)
                att = jnp.where(lvl == l + 1, a, att)
            qt = x_scr[nlev, rows, :].astype(BF16)
            kt = x_scr[nlev + 1, rows, :].astype(BF16)
            vb = v_ref[blk_rows, :].astype(BF16)
            st0 = st_scr[h]
            o = lax.dot_general(qt, st0.astype(BF16), NT, preferred_element_type=F32)
            o = o + jnp.dot(att.astype(BF16), vb, preferred_element_type=F32)
            o_ref[blk_rows, :] = o + d_scr[rows, :]
            st_scr[h] = st0 * p_last[h:h + 1, :] + lax.dot_general(vb, kt, TN,
                                                                   preferred_element_type=F32)

        o3 = to3(o_ref[whole, :])
        o3 = o3 * lax.rsqrt(jnp.mean(o3 * o3, axis=-1, keepdims=True) + EPS)
        o_ref[whole, :] = flat(o3 * nw_ref[...] * to3(ga_ref[whole, :]))
        return carry

    lax.fori_loop(0, q_ref.shape[0] // rows_chunk, chunk_body, 0)

    @pl.when(c_idx == pl.num_programs(1) - 1)
    def _():
        for h in range(heads):
            s_out_ref[h] = st_scr[h].T


def _recur(q, f, v, ga, norm_w, *, n, heads, chunk, chunks_per_step):
    rows_total = q.shape[0]
    l = rows_total // (n * heads)
    nlev = chunk.bit_length() - 1
    rows = chunk * heads
    steps = l // (chunk * chunks_per_step)
    blk = pl.BlockSpec((chunks_per_step * rows, HEAD_DIM), lambda i, c: (i * steps + c, 0))
    st_spec = pl.BlockSpec((None, None, heads, HEAD_DIM, HEAD_DIM), lambda i, c: (0, i, 0, 0, 0))
    lvl = jnp.asarray(_split_level_table(chunk))
    return pl.pallas_call(
        functools.partial(_recur_kernel, heads=heads, chunk=chunk),
        grid=(n, steps),
        in_specs=[blk, blk, blk, blk, _const_spec((heads, HEAD_DIM)), _const_spec((chunk, chunk))],
        out_specs=[blk, st_spec],
        out_shape=[jax.ShapeDtypeStruct((rows_total, HEAD_DIM), F32),
                   jax.ShapeDtypeStruct((1, n, heads, HEAD_DIM, HEAD_DIM), F32)],
        scratch_shapes=[pltpu.VMEM((heads, HEAD_DIM, HEAD_DIM), F32),
                        pltpu.VMEM((nlev + 2, rows, HEAD_DIM), F32),
                        pltpu.VMEM((rows, HEAD_DIM), F32)],
        compiler_params=_params("parallel", "arbitrary"),
        name="recur",
    )(q, f, v, ga, norm_w.reshape(heads, HEAD_DIM), lvl)


def _recur_step_kernel(q_ref, f_ref, v_ref, ga_ref, nw_ref, s0_ref, o_ref, s_out_ref,
                       qt_scr, kt_scr, vp_scr, oi_scr, pt_scr, *, heads, steps, seqs, pad):
    rows_seq = steps * heads
    zero = jnp.zeros((pad * heads, HEAD_DIM), F32)
    qt_scr[...] = zero
    kt_scr[...] = zero
    vp_scr[...] = zero
    pt_scr[...] = jnp.zeros_like(pt_scr)

    def seq_body(g, carry):
        base = pl.multiple_of(g * rows_seq, rows_seq)
        blk = pl.ds(base, rows_seq)
        to3 = lambda a: a.reshape(steps, heads, HEAD_DIM)
        q3, f3, v3 = to3(q_ref[blk, :]), to3(f_ref[blk, :]), to3(v_ref[blk, :])
        k3 = 1.0 - f3
        p = [f3[0]]
        for t in range(1, steps):
            p.append(p[-1] * f3[t])
        r = [None] * steps
        r[steps - 1] = jnp.ones_like(f3[0])
        for s in range(steps - 2, -1, -1):
            r[s] = r[s + 1] * f3[s + 1]
        intra = []
        for t in range(steps):
            acc = jnp.sum(q3[t] * k3[t], axis=-1, keepdims=True) * v3[t]
            dec = None
            for s in range(t - 1, -1, -1):
                dec = f3[s + 1] if dec is None else dec * f3[s + 1]
                acc = acc + jnp.sum(q3[t] * k3[s] * dec, axis=-1, keepdims=True) * v3[s]
            intra.append(acc)
        for t in range(steps):
            rows_t = pl.ds(t * heads, heads)
            qt_scr[rows_t, :] = q3[t] * p[t]
            kt_scr[rows_t, :] = k3[t] * r[t]
            vp_scr[rows_t, :] = v3[t]
        pt_scr[0:heads, :] = p[steps - 1]
        p_col = pt_scr[...].T
        for h in range(heads):
            rows = _group_rows(h, pad, heads)
            qt = qt_scr[rows, :].astype(BF16)
            kt = kt_scr[rows, :].astype(BF16)
            vb = vp_scr[rows, :].astype(BF16)
            s0 = s0_ref[g, h]
            oi_scr[rows, :] = jnp.dot(qt, s0.astype(BF16), preferred_element_type=F32)
            s_out_ref[g, h] = s0 * p_col[:, h:h + 1] + lax.dot_general(
                kt, vb, TN, preferred_element_type=F32)
        o3 = to3(oi_scr[0:rows_seq, :]) + jnp.stack(intra, axis=0)
        o3 = o3 * lax.rsqrt(jnp.mean(o3 * o3, axis=-1, keepdims=True) + EPS)
        o_ref[blk, :] = (o3 * nw_ref[...] * to3(ga_ref[blk, :])).reshape(rows_seq, HEAD_DIM)
        return carry

    lax.fori_loop(0, seqs, seq_body, 0)


def _recur_step(q, f, v, ga, norm_w, states, *, layer, heads, steps, seqs, pad):
    n = states.shape[1]
    rows = seqs * steps * heads
    blk = pl.BlockSpec((rows, HEAD_DIM), lambda i: (i, 0))
    st_blk = (None, seqs, heads, HEAD_DIM, HEAD_DIM)
    scr = pltpu.VMEM((pad * heads, HEAD_DIM), F32)
    return pl.pallas_call(
        functools.partial(_recur_step_kernel, heads=heads, steps=steps, seqs=seqs, pad=pad),
        grid=(n // seqs,),
        in_specs=[blk, blk, blk, blk, _const_spec((heads, HEAD_DIM)),
                  pl.BlockSpec(st_blk, lambda i: (layer, i, 0, 0, 0))],
        out_specs=[blk, pl.BlockSpec(st_blk, lambda i: (0, i, 0, 0, 0))],
        out_shape=[jax.ShapeDtypeStruct(q.shape, F32), jax.ShapeDtypeStruct((1,) + states.shape[1:], F32)],
        scratch_shapes=[scr, scr, scr, scr, pltpu.VMEM((HEAD_DIM, HEAD_DIM), F32)],
        compiler_params=_params("parallel"),
        name="recur_step",
    )(q, f, v, ga, norm_w.reshape(heads, HEAD_DIM), states)


def _conv_taps(ext_scr, w_ref, b_ref, y_ref, *, ext_row0, out_row0, tb, groups):
    acc = [jnp.broadcast_to(b_ref[...], (tb, groups, LANES)), jnp.zeros((tb, groups, LANES), F32)]
    for j in range(CONV_WIDTH):
        start = (ext_row0 + j) * groups
        if not isinstance(start, int):
            start = pl.multiple_of(start, groups)
        e = ext_scr[pl.ds(start, tb * groups), :].reshape(tb, groups, LANES)
        acc[j % 2] = acc[j % 2] + w_ref[j] * e
    y_ref[pl.ds(out_row0 * groups, tb * groups), :] = (acc[0] + acc[1]).reshape(tb * groups, LANES)


def _conv_kernel(u_ref, w_ref, b_ref, y_ref, buf_out_ref, ext_scr, *, tile, tb, groups):
    t_idx = pl.program_id(1)
    hist = CONV_HIST * groups

    @pl.when(t_idx == 0)
    def _():
        ext_scr[0:hist, :] = jnp.zeros((hist, LANES), F32)

    @pl.when(t_idx > 0)
    def _():
        ext_scr[0:hist, :] = ext_scr[tile * groups:tile * groups + hist, :]

    ext_scr[hist:hist + tile * groups, :] = u_ref[...]

    def block_body(i, carry):
        r0 = pl.multiple_of(i * tb, tb)
        _conv_taps(ext_scr, w_ref, b_ref, y_ref, ext_row0=r0, out_row0=r0, tb=tb, groups=groups)
        return carry

    lax.fori_loop(0, tile // tb, block_body, 0)

    @pl.when(t_idx == pl.num_programs(1) - 1)
    def _():
        for g in range(groups):
            buf_out_ref[:, g * LANES:(g + 1) * LANES] = ext_scr[
                pl.ds(tile * groups + g, CONV_HIST, stride=groups), :]


def _conv(u, dw_w, dw_b, *, n, tile, tb):
    ch = dw_w.shape[1]
    groups = ch // LANES
    l = u.shape[0] // (n * groups)
    steps = l // tile
    blk = pl.BlockSpec((tile * groups, LANES), lambda i, t: (i * steps + t, 0))
    return pl.pallas_call(
        functools.partial(_conv_kernel, tile=tile, tb=tb, groups=groups),
        grid=(n, steps),
        in_specs=[blk, _const_spec((CONV_WIDTH, groups, LANES)), _const_spec((1, groups, LANES))],
        out_specs=[blk, pl.BlockSpec((None, None, CONV_HIST, ch), lambda i, t: (0, i, 0, 0))],
        out_shape=[jax.ShapeDtypeStruct(u.shape, F32), jax.ShapeDtypeStruct((1, n, CONV_HIST, ch), F32)],
        scratch_shapes=[pltpu.VMEM(((tile + CONV_HIST) * groups, LANES), F32)],
        compiler_params=_params("parallel", "arbitrary"),
        name="conv",
    )(u, dw_w.reshape(CONV_WIDTH, groups, LANES), dw_b.reshape(1, groups, LANES))


def _conv_step_kernel(u_ref, w_ref, b_ref, buf_ref, y_ref, buf_out_ref, ext_scr, *, steps, seqs, groups):
    hist = CONV_HIST * groups
    rows_seq = steps * groups
    for s in range(seqs):
        for g in range(groups):
            ext_scr[pl.ds(g, CONV_HIST, stride=groups), :] = buf_ref[s, :, g * LANES:(g + 1) * LANES]
        ext_scr[hist:hist + rows_seq, :] = u_ref[s * rows_seq:(s + 1) * rows_seq, :]
        _conv_taps(ext_scr, w_ref, b_ref, y_ref, ext_row0=0, out_row0=s * steps, tb=steps, groups=groups)
        for g in range(groups):
            buf_out_ref[s, :, g * LANES:(g + 1) * LANES] = ext_scr[
                pl.ds(rows_seq + g, CONV_HIST, stride=groups), :]


def _conv_step(u, dw_w, dw_b, bufs, *, layer, steps, seqs):
    _, n, _, ch = bufs.shape
    groups = ch // LANES
    blk = pl.BlockSpec((seqs * steps * groups, LANES), lambda i: (i, 0))
    buf_blk = (None, seqs, CONV_HIST, ch)
    return pl.pallas_call(
        functools.partial(_conv_step_kernel, steps=steps, seqs=seqs, groups=groups),
        grid=(n // seqs,),
        in_specs=[blk, _const_spec((CONV_WIDTH, groups, LANES)), _const_spec((1, groups, LANES)),
                  pl.BlockSpec(buf_blk, lambda i: (layer, i, 0, 0))],
        out_specs=[blk, pl.BlockSpec(buf_blk, lambda i: (0, i, 0, 0))],
        out_shape=[jax.ShapeDtypeStruct(u.shape, F32), jax.ShapeDtypeStruct((1,) + bufs.shape[1:], F32)],
        scratch_shapes=[pltpu.VMEM(((steps + CONV_HIST) * groups, LANES), F32)],
        compiler_params=_params("parallel"),
        name="conv_step",
    )(u, dw_w.reshape(CONV_WIDTH, groups, LANES), dw_b.reshape(1, groups, LANES), bufs)


def _post_kernel(x_ref, oa_ref, yc_ref, sgb_ref, g1_ref, sh2_ref, sc2_ref, g2_ref, lnw_ref, lnb_ref,
                 wo_ref, nfw_ref, w1_ref, w2_ref, nlw_ref, y_ref, m_scr, c_scr, h_scr, acc_scr, *, fc):
    tm, d = x_ref.shape
    groups = d // LANES
    for g in range(groups):
        rows = _group_rows(g, tm, groups)
        cols = slice(g * LANES, (g + 1) * LANES)
        m_scr[:, cols] = oa_ref[rows, :].astype(BF16)
        c_scr[:, cols] = yc_ref[rows, :]
    yc = c_scr[...]
    mu = jnp.mean(yc, axis=-1, keepdims=True)
    dv = yc - mu
    var = jnp.mean(dv * dv, axis=-1, keepdims=True)
    ob = _silu(dv * lax.rsqrt(var + EPS) * lnw_ref[...] + lnb_ref[...])
    for g in range(groups):
        cols = slice(g * LANES, (g + 1) * LANES)
        m_scr[:, d + g * LANES:d + (g + 1) * LANES] = (
            ob[:, cols] * sgb_ref[_group_rows(g, tm, groups), :]).astype(BF16)
    merged = jnp.dot(m_scr[...], wo_ref[...], preferred_element_type=F32)
    x1 = x_ref[...] + g1_ref[...] * merged
    ms = jnp.mean(x1 * x1, axis=-1, keepdims=True)
    h2 = x1 * lax.rsqrt(ms + EPS) * nfw_ref[...]
    h2 = h2 * (1.0 + sc2_ref[...]) + sh2_ref[...]
    h_scr[...] = h2.astype(BF16)
    d_ff = w2_ref.shape[0]
    for c in range(d_ff // fc):
        gate = jnp.dot(h_scr[...], w1_ref[:, c * fc:(c + 1) * fc], preferred_element_type=F32)
        up = jnp.dot(h_scr[...], w1_ref[:, d_ff + c * fc:d_ff + (c + 1) * fc], preferred_element_type=F32)
        part = jnp.dot((_silu(gate) * up).astype(BF16), w2_ref[c * fc:(c + 1) * fc, :],
                       preferred_element_type=F32)
        if c == 0:
            acc_scr[...] = part
        else:
            acc_scr[...] += part
    x2 = x1 + g2_ref[...] * acc_scr[...]
    ms2 = jnp.mean(x2 * x2, axis=-1, keepdims=True)
    y_ref[...] = x2 * lax.rsqrt(ms2 + EPS) * nlw_ref[...]


def _post(x, oa, yc, sgb, mod, lnw, lnb, wo, nfw, w1, w2, nlw, *, tm, fc, rows_per_mod):
    m, d = x.shape
    groups = d // LANES
    mod_spec = lambda row: _mod_spec(mod, row, d, tm, rows_per_mod)
    row_spec = pl.BlockSpec((tm, d), lambda i: (i, 0))
    il_spec = pl.BlockSpec((tm * groups, LANES), lambda i: (i, 0))
    vec = _const_spec((1, d))
    return pl.pallas_call(
        functools.partial(_post_kernel, fc=fc),
        grid=(m // tm,),
        in_specs=[row_spec, il_spec, il_spec, il_spec, mod_spec(2), mod_spec(3), mod_spec(4), mod_spec(5),
                  vec, vec, _const_spec(wo.shape), vec, _const_spec(w1.shape), _const_spec(w2.shape), vec],
        out_specs=row_spec,
        out_shape=jax.ShapeDtypeStruct((m, d), F32),
        scratch_shapes=[pltpu.VMEM((tm, 2 * d), BF16), pltpu.VMEM((tm, d), F32),
                        pltpu.VMEM((tm, d), BF16), pltpu.VMEM((tm, d), F32)],
        compiler_params=_params("parallel"),
        name="post",
    )(x, oa, yc, sgb, mod, mod, mod, mod, lnw, lnb, wo, nfw, w1, w2, nlw)


INPROJ_TC = 256
FFN_CHUNK = 256
MOD_TN = 1024
PROMPT_TM = 512
PROMPT_CHUNK = 128
PROMPT_CHUNKS_PER_STEP = 4
PROMPT_CONV_TILE = 512
PROMPT_CONV_TB = 16
SAMPLE_SEQS = 16
SAMPLE_PAD = 16


def kernel(x_prompt, x_sample, state_hgrn, state_conv, c_prompt, c_sample, w_ada, b_ada, norm_mix_w, w_in, b_in, lb_logits, hgrn_norm_w, conv_dw_w, conv_dw_b, conv_ln_w, conv_ln_b, w_out, norm_ffn_w, w_ffn_in, w_ffn_out, norm_final_w):
    depth = w_in.shape[0]
    assert depth == 1, "single-layer configuration"
    layer = 0
    n_p, l_p, d = x_prompt.shape
    n_s, l_s, _ = x_sample.shape
    heads = d // HEAD_DIM

    w_in_b = w_in[layer].astype(BF16)
    w1 = w_ffn_in[layer].astype(BF16)
    w2 = w_ffn_out[layer].astype(BF16)
    wo = w_out[layer].astype(BF16)
    row = lambda a: a.reshape(1, -1)

    mod = _mod(jnp.concatenate([c_prompt, c_sample], axis=0), w_ada[layer], b_ada[layer], tn=MOD_TN)
    mod_p = mod[:n_p].reshape(n_p, 1, MOD_ROWS * d)
    mod_s = jnp.repeat(mod[n_p:], l_s, axis=0).reshape(1, n_s * l_s, MOD_ROWS * d)

    def inproj(x2, mod_g, tm):
        return _inproj(x2, mod_g, row(norm_mix_w[layer]), w_in_b, row(b_in[layer]), lb_logits,
                       layer=layer, tm=tm, tc=INPROJ_TC, rows_per_mod=tm if mod_g.shape[1] > 1 else l_p)

    def post(x2, oa, yc, sgb, mod_g, tm):
        return _post(x2, oa, yc, sgb, mod_g, row(conv_ln_w[layer]), row(conv_ln_b[layer]), wo,
                     row(norm_ffn_w[layer]), w1, w2, row(norm_final_w),
                     tm=tm, fc=FFN_CHUNK, rows_per_mod=tm if mod_g.shape[1] > 1 else l_p)

    xp = x_prompt.reshape(n_p * l_p, d)
    q, f, v, ga, u, sgb = inproj(xp, mod_p, PROMPT_TM)
    oa, s_p = _recur(q, f, v, ga, hgrn_norm_w[layer], n=n_p, heads=heads, chunk=PROMPT_CHUNK,
                     chunks_per_step=PROMPT_CHUNKS_PER_STEP)
    yc, b_p = _conv(u, conv_dw_w[layer], conv_dw_b[layer], n=n_p, tile=PROMPT_CONV_TILE, tb=PROMPT_CONV_TB)
    y_p = post(xp, oa, yc, sgb, mod_p, PROMPT_TM).reshape(n_p, l_p, d)

    xs = x_sample.reshape(n_s * l_s, d)
    q, f, v, ga, u, sgb = inproj(xs, mod_s, n_s * l_s)
    oa, s_s = _recur_step(q, f, v, ga, hgrn_norm_w[layer], state_hgrn, layer=layer,
                          heads=heads, steps=l_s, seqs=SAMPLE_SEQS, pad=SAMPLE_PAD)
    yc, b_s = _conv_step(u, conv_dw_w[layer], conv_dw_b[layer], state_conv, layer=layer,
                         steps=l_s, seqs=SAMPLE_SEQS)
    y_s = post(xs, oa, yc, sgb, mod_s, n_s * l_s).reshape(n_s, l_s, d)

    return (y_p, y_s, s_p, b_p, s_s, b_s)
```

```python
import functools

import numpy as np
import jax
import jax.numpy as jnp
from jax import lax
from jax.experimental import pallas as pl
from jax.experimental.pallas import tpu as pltpu

F32 = jnp.float32
BF16 = jnp.bfloat16

EPS = 1e-6
LANES = 128
SUBLANES = 8
HEAD_DIM = LANES
CONV_WIDTH = 31
CONV_HIST = CONV_WIDTH - 1
VMEM_LIMIT_BYTES = 56 * 1024 * 1024

IN_SEGMENTS = 8
MOD_ROWS = 6

NT = (((1,), (1,)), ((), ()))
TN = (((0,), (0,)), ((), ()))


def _silu(x):
    return x * jax.nn.sigmoid(x)


def _params(*sem):
    return pltpu.CompilerParams(dimension_semantics=sem, vmem_limit_bytes=VMEM_LIMIT_BYTES)


def _const_spec(shape):
    nd = len(shape)
    return pl.BlockSpec(shape, lambda *_: (0,) * nd, pipeline_mode=pl.Buffered(1))


def _group_rows(g, n, groups):
    return pl.ds(g, n, stride=groups)


def _mod_kernel(c_ref, w_ref, b_ref, o_ref):
    a = _silu(c_ref[...]).astype(BF16)
    o_ref[...] = jnp.dot(a, w_ref[...].astype(BF16), preferred_element_type=F32) + b_ref[...]


def _mod(c, w_ada, b_ada, *, tn=512):
    n, d = c.shape
    cols = w_ada.shape[1]
    return pl.pallas_call(
        _mod_kernel,
        grid=(cols // tn,),
        in_specs=[pl.BlockSpec((n, d), lambda j: (0, 0)),
                  pl.BlockSpec((d, tn), lambda j: (0, j)),
                  pl.BlockSpec((1, tn), lambda j: (0, j))],
        out_specs=pl.BlockSpec((n, tn), lambda j: (0, j)),
        out_shape=jax.ShapeDtypeStruct((n, cols), F32),
        compiler_params=_params("parallel"),
        name="mod",
    )(c, w_ada, b_ada.reshape(1, cols))


def _inproj_kernel(x_ref, sh_ref, sc_ref, nw_ref, w_ref, b_ref, lbl_ref,
                   q_ref, f_ref, v_ref, ga_ref, u_ref, sgb_ref, h_scr, *, layer, tc):
    tm, d = x_ref.shape
    groups = d // LANES
    x = x_ref[...]
    ms = jnp.mean(x * x, axis=-1, keepdims=True)
    h = x * lax.rsqrt(ms + EPS) * nw_ref[...]
    h = h * (1.0 + sc_ref[...]) + sh_ref[...]
    h_scr[...] = h.astype(BF16)
    for j in range(d // tc):
        def seg(s):
            cols = slice(s * d + j * tc, s * d + (j + 1) * tc)
            return jnp.dot(h_scr[...], w_ref[:, cols], preferred_element_type=F32) + b_ref[:, cols]

        def put(ref, val):
            for g in range(tc // LANES):
                rows = _group_rows(j * (tc // LANES) + g, tm, groups)
                ref[rows, :] = val[:, g * LANES:(g + 1) * LANES]

        put(q_ref, _silu(seg(0)))
        lg = lbl_ref[:, j * tc:(j + 1) * tc]
        e = jnp.exp(lg - jnp.max(lg, axis=0, keepdims=True))
        lb = jnp.sum(e[:layer + 1], axis=0, keepdims=True) / jnp.sum(e, axis=0, keepdims=True)
        put(f_ref, lb + (1.0 - lb) * jax.nn.sigmoid(seg(1)))
        put(v_ref, seg(2))
        og_act = _silu(seg(3))
        ga_ref[:, j * tc:(j + 1) * tc] = og_act * jax.nn.sigmoid(seg(6))
        glu_a = seg(4)
        put(u_ref, glu_a * jax.nn.sigmoid(seg(5)))
        sgb_ref[:, j * tc:(j + 1) * tc] = jax.nn.sigmoid(seg(7)).astype(sgb_ref.dtype)


def _mod_spec(mod, row, d, tm, rows_per_mod):
    tiles_per_mod = rows_per_mod // tm
    return pl.BlockSpec((None, mod.shape[1], d), lambda i: (i // tiles_per_mod, 0, row))


def _inproj(x, mod, nw, w, b, lbl, *, layer, tm, tc, rows_per_mod):
    m, d = x.shape
    groups = d // LANES
    mod_spec = lambda row: _mod_spec(mod, row, d, tm, rows_per_mod)
    row_spec = pl.BlockSpec((tm, d), lambda i: (i, 0))
    il_spec = pl.BlockSpec((tm * groups, LANES), lambda i: (i, 0))
    il_shape = jax.ShapeDtypeStruct((m * groups, LANES), F32)
    return pl.pallas_call(
        functools.partial(_inproj_kernel, layer=layer, tc=tc),
        grid=(m // tm,),
        in_specs=[row_spec, mod_spec(0), mod_spec(1),
                  _const_spec((1, d)), _const_spec(w.shape), _const_spec(b.shape),
                  _const_spec(lbl.shape)],
        out_specs=[il_spec, il_spec, il_spec, row_spec, il_spec, row_spec],
        out_shape=[il_shape, il_shape, il_shape, jax.ShapeDtypeStruct((m, d), F32), il_shape,
                   jax.ShapeDtypeStruct((m, d), BF16)],
        scratch_shapes=[pltpu.VMEM((tm, d), BF16)],
        compiler_params=_params("parallel"),
        name="inproj",
    )(x, mod, mod, nw, w, b, lbl)


def _split_level_table(c):
    t = np.arange(c)[:, None]
    s = np.arange(c)[None, :]
    x = t ^ s
    lvl = np.zeros((c, c), np.int32)
    for l in range(1, c.bit_length()):
        lvl[(x >> (l - 1)) == 1] = l
    return np.where(t > s, lvl, 0).astype(np.int32)


def _recur_kernel(q_ref, f_ref, v_ref, ga_ref, nw_ref, lvl_ref, o_ref, s_out_ref,
                  st_scr, x_scr, d_scr, *, heads, chunk):
    c_idx = pl.program_id(1)
    nlev = chunk.bit_length() - 1

    @pl.when(c_idx == 0)
    def _():
        st_scr[...] = jnp.zeros_like(st_scr)

    def to3(a):
        return a.reshape(chunk, heads, HEAD_DIM)

    def flat(a):
        return a.reshape(chunk * heads, HEAD_DIM)

    lvl = lvl_ref[...]
    rows_chunk = chunk * heads

    def chunk_body(ci, carry):
        base = pl.multiple_of(ci * rows_chunk, rows_chunk)
        whole = pl.ds(base, rows_chunk)
        steps = pl.ds(pl.multiple_of(ci * chunk, chunk), chunk)
        f3 = to3(f_ref[whole, :])
        q3 = to3(q_ref[whole, :])
        v3 = to3(v_ref[whole, :])
        k3 = 1.0 - f3
        p, r = f3, jnp.ones_like(f3)
        half = 1
        for l in range(nlev):
            blk = 2 * half
            shp = (chunk // blk, blk, heads, HEAD_DIM)
            p4, r4, q4, k4 = (a.reshape(shp) for a in (p, r, q3, k3))
            x = jnp.concatenate([k4[:, :half] * r4[:, :half], q4[:, half:] * p4[:, half:]], axis=1)
            x_scr[l] = flat(x)
            left_tot = p4[:, half - 1:half]
            right_tot = p4[:, blk - 1:blk]
            p = jnp.concatenate([p4[:, :half], p4[:, half:] * left_tot], axis=1).reshape(f3.shape)
            r = jnp.concatenate([r4[:, :half] * right_tot, r4[:, half:]], axis=1).reshape(f3.shape)
            half = blk
        x_scr[nlev] = flat(q3 * p)
        x_scr[nlev + 1] = flat(k3 * r)
        d_scr[...] = flat(jnp.sum(q3 * k3, axis=-1, keepdims=True) * v3)
        p_last = p[chunk - 1]

        for h in range(heads):
            rows = _group_rows(h, chunk, heads)
            blk_rows = _group_rows(base + h, chunk, heads)
            att = jnp.zeros((chunk, chunk), F32)
            for l in range(nlev):
                xb = x_scr[l, rows, :].astype(BF16)
                a = lax.dot_general(xb, xb, NT, preferred_element_type=F32)
                att = jnp.where(lvl == l + 1, a, att)
            qt = x_scr[nlev, rows, :].astype(BF16)
            kt = x_scr[nlev + 1, rows, :].astype(BF16)
            vb = v_ref[blk_rows, :].astype(BF16)
            st0 = st_scr[h]
            o = lax.dot_general(qt, st0.astype(BF16), NT, preferred_element_type=F32)
            o = o + jnp.dot(att.astype(BF16), vb, preferred_element_type=F32) + d_scr[rows, :]
            st_scr[h] = st0 * p_last[h:h + 1, :] + lax.dot_general(vb, kt, TN,
                                                                   preferred_element_type=F32)
            o = o * lax.rsqrt(jnp.mean(o * o, axis=-1, keepdims=True) + EPS)
            cols = slice(h * HEAD_DIM, (h + 1) * HEAD_DIM)
            o_ref[steps, cols] = o * nw_ref[h:h + 1, :] * ga_ref[steps, cols]
        return carry

    lax.fori_loop(0, q_ref.shape[0] // rows_chunk, chunk_body, 0)

    @pl.when(c_idx == pl.num_programs(1) - 1)
    def _():
        for h in range(heads):
            s_out_ref[h] = st_scr[h].T


def _recur(q, f, v, ga, norm_w, *, n, heads, chunk, chunks_per_step):
    rows_total = q.shape[0]
    l = rows_total // (n * heads)
    nlev = chunk.bit_length() - 1
    rows = chunk * heads
    steps = l // (chunk * chunks_per_step)
    blk = pl.BlockSpec((chunks_per_step * rows, HEAD_DIM), lambda i, c: (i * steps + c, 0))
    row_blk = pl.BlockSpec((chunks_per_step * chunk, heads * HEAD_DIM), lambda i, c: (i * steps + c, 0))
    st_spec = pl.BlockSpec((None, None, heads, HEAD_DIM, HEAD_DIM), lambda i, c: (0, i, 0, 0, 0))
    lvl = jnp.asarray(_split_level_table(chunk))
    return pl.pallas_call(
        functools.partial(_recur_kernel, heads=heads, chunk=chunk),
        grid=(n, steps),
        in_specs=[blk, blk, blk, row_blk, _const_spec((heads, HEAD_DIM)), _const_spec((chunk, chunk))],
        out_specs=[row_blk, st_spec],
        out_shape=[jax.ShapeDtypeStruct(ga.shape, F32),
                   jax.ShapeDtypeStruct((1, n, heads, HEAD_DIM, HEAD_DIM), F32)],
        scratch_shapes=[pltpu.VMEM((heads, HEAD_DIM, HEAD_DIM), F32),
                        pltpu.VMEM((nlev + 2, rows, HEAD_DIM), F32),
                        pltpu.VMEM((rows, HEAD_DIM), F32)],
        compiler_params=_params("parallel", "arbitrary"),
        name="recur",
    )(q, f, v, ga, norm_w.reshape(heads, HEAD_DIM), lvl)


def _recur_step_kernel(q_ref, f_ref, v_ref, ga_ref, nw_ref, s0_ref, o_ref, s_out_ref,
                       qt_scr, kt_scr, vp_scr, oi_scr, pt_scr, g_scr, *, heads, steps, seqs, pad):
    rows_seq = steps * heads
    zero = jnp.zeros((pad * heads, HEAD_DIM), F32)
    qt_scr[...] = zero
    kt_scr[...] = zero
    vp_scr[...] = zero
    pt_scr[...] = jnp.zeros_like(pt_scr)
    for h in range(heads):
        g_scr[_group_rows(h, seqs * steps, heads), :] = ga_ref[:, h * HEAD_DIM:(h + 1) * HEAD_DIM]

    def seq_body(g, carry):
        base = pl.multiple_of(g * rows_seq, rows_seq)
        blk = pl.ds(base, rows_seq)
        to3 = lambda a: a.reshape(steps, heads, HEAD_DIM)
        q3, f3, v3 = to3(q_ref[blk, :]), to3(f_ref[blk, :]), to3(v_ref[blk, :])
        k3 = 1.0 - f3
        p = [f3[0]]
        for t in range(1, steps):
            p.append(p[-1] * f3[t])
        r = [None] * steps
        r[steps - 1] = jnp.ones_like(f3[0])
        for s in range(steps - 2, -1, -1):
            r[s] = r[s + 1] * f3[s + 1]
        intra = []
        for t in range(steps):
            acc = jnp.sum(q3[t] * k3[t], axis=-1, keepdims=True) * v3[t]
            dec = None
            for s in range(t - 1, -1, -1):
                dec = f3[s + 1] if dec is None else dec * f3[s + 1]
                acc = acc + jnp.sum(q3[t] * k3[s] * dec, axis=-1, keepdims=True) * v3[s]
            intra.append(acc)
        for t in range(steps):
            rows_t = pl.ds(t * heads, heads)
            qt_scr[rows_t, :] = q3[t] * p[t]
            kt_scr[rows_t, :] = k3[t] * r[t]
            vp_scr[rows_t, :] = v3[t]
        pt_scr[0:heads, :] = p[steps - 1]
        p_col = pt_scr[...].T
        for h in range(heads):
            rows = _group_rows(h, pad, heads)
            qt = qt_scr[rows, :].astype(BF16)
            kt = kt_scr[rows, :].astype(BF16)
            vb = vp_scr[rows, :].astype(BF16)
            s0 = s0_ref[g, h]
            oi_scr[rows, :] = jnp.dot(qt, s0.astype(BF16), preferred_element_type=F32)
            s_out_ref[g, h] = s0 * p_col[:, h:h + 1] + lax.dot_general(
                kt, vb, TN, preferred_element_type=F32)
        o3 = to3(oi_scr[0:rows_seq, :]) + jnp.stack(intra, axis=0)
        o3 = o3 * lax.rsqrt(jnp.mean(o3 * o3, axis=-1, keepdims=True) + EPS)
        g_scr[blk, :] = (o3 * nw_ref[...] * to3(g_scr[blk, :])).reshape(rows_seq, HEAD_DIM)
        return carry

    lax.fori_loop(0, seqs, seq_body, 0)
    for h in range(heads):
        o_ref[:, h * HEAD_DIM:(h + 1) * HEAD_DIM] = g_scr[_group_rows(h, seqs * steps, heads), :]


def _recur_step(q, f, v, ga, norm_w, states, *, layer, heads, steps, seqs, pad):
    n = states.shape[1]
    rows = seqs * steps * heads
    blk = pl.BlockSpec((rows, HEAD_DIM), lambda i: (i, 0))
    row_blk = pl.BlockSpec((seqs * steps, heads * HEAD_DIM), lambda i: (i, 0))
    st_blk = (None, seqs, heads, HEAD_DIM, HEAD_DIM)
    scr = pltpu.VMEM((pad * heads, HEAD_DIM), F32)
    return pl.pallas_call(
        functools.partial(_recur_step_kernel, heads=heads, steps=steps, seqs=seqs, pad=pad),
        grid=(n // seqs,),
        in_specs=[blk, blk, blk, row_blk, _const_spec((heads, HEAD_DIM)),
                  pl.BlockSpec(st_blk, lambda i: (layer, i, 0, 0, 0))],
        out_specs=[row_blk, pl.BlockSpec(st_blk, lambda i: (0, i, 0, 0, 0))],
        out_shape=[jax.ShapeDtypeStruct(ga.shape, F32), jax.ShapeDtypeStruct((1,) + states.shape[1:], F32)],
        scratch_shapes=[scr, scr, scr, scr, pltpu.VMEM((HEAD_DIM, HEAD_DIM), F32),
                        pltpu.VMEM((rows, HEAD_DIM), F32)],
        compiler_params=_params("parallel"),
        name="recur_step",
    )(q, f, v, ga, norm_w.reshape(heads, HEAD_DIM), states)


def _conv_taps(ext_scr, w_ref, b_ref, y_ref, *, ext_row0, out_row0, tb, groups):
    acc = [jnp.broadcast_to(b_ref[...], (tb, groups, LANES)), jnp.zeros((tb, groups, LANES), F32)]
    for j in range(CONV_WIDTH):
        start = (ext_row0 + j) * groups
        if not isinstance(start, int):
            start = pl.multiple_of(start, groups)
        e = ext_scr[pl.ds(start, tb * groups), :].reshape(tb, groups, LANES)
        acc[j % 2] = acc[j % 2] + w_ref[j] * e
    y_ref[pl.ds(out_row0 * groups, tb * groups), :] = (acc[0] + acc[1]).reshape(tb * groups, LANES)


def _conv_kernel(u_ref, w_ref, b_ref, y_ref, buf_out_ref, ext_scr, *, tile, tb, groups):
    t_idx = pl.program_id(1)
    hist = CONV_HIST * groups

    @pl.when(t_idx == 0)
    def _():
        ext_scr[0:hist, :] = jnp.zeros((hist, LANES), F32)

    @pl.when(t_idx > 0)
    def _():
        ext_scr[0:hist, :] = ext_scr[tile * groups:tile * groups + hist, :]

    ext_scr[hist:hist + tile * groups, :] = u_ref[...]

    def block_body(i, carry):
        r0 = pl.multiple_of(i * tb, tb)
        _conv_taps(ext_scr, w_ref, b_ref, y_ref, ext_row0=r0, out_row0=r0, tb=tb, groups=groups)
        return carry

    lax.fori_loop(0, tile // tb, block_body, 0)

    @pl.when(t_idx == pl.num_programs(1) - 1)
    def _():
        for g in range(groups):
            buf_out_ref[:, g * LANES:(g + 1) * LANES] = ext_scr[
                pl.ds(tile * groups + g, CONV_HIST, stride=groups), :]


def _conv(u, dw_w, dw_b, *, n, tile, tb):
    ch = dw_w.shape[1]
    groups = ch // LANES
    l = u.shape[0] // (n * groups)
    steps = l // tile
    blk = pl.BlockSpec((tile * groups, LANES), lambda i, t: (i * steps + t, 0))
    return pl.pallas_call(
        functools.partial(_conv_kernel, tile=tile, tb=tb, groups=groups),
        grid=(n, steps),
        in_specs=[blk, _const_spec((CONV_WIDTH, groups, LANES)), _const_spec((1, groups, LANES))],
        out_specs=[blk, pl.BlockSpec((None, None, CONV_HIST, ch), lambda i, t: (0, i, 0, 0))],
        out_shape=[jax.ShapeDtypeStruct(u.shape, F32), jax.ShapeDtypeStruct((1, n, CONV_HIST, ch), F32)],
        scratch_shapes=[pltpu.VMEM(((tile + CONV_HIST) * groups, LANES), F32)],
        compiler_params=_params("parallel", "arbitrary"),
        name="conv",
    )(u, dw_w.reshape(CONV_WIDTH, groups, LANES), dw_b.reshape(1, groups, LANES))


def _conv_step_kernel(u_ref, w_ref, b_ref, buf_ref, y_ref, buf_out_ref, ext_scr, *, steps, seqs, groups):
    hist = CONV_HIST * groups
    rows_seq = steps * groups
    for s in range(seqs):
        for g in range(groups):
            ext_scr[pl.ds(g, CONV_HIST, stride=groups), :] = buf_ref[s, :, g * LANES:(g + 1) * LANES]
        ext_scr[hist:hist + rows_seq, :] = u_ref[s * rows_seq:(s + 1) * rows_seq, :]
        _conv_taps(ext_scr, w_ref, b_ref, y_ref, ext_row0=0, out_row0=s * steps, tb=steps, groups=groups)
        for g in range(groups):
            buf_out_ref[s, :, g * LANES:(g + 1) * LANES] = ext_scr[
                pl.ds(rows_seq + g, CONV_HIST, stride=groups), :]


def _conv_step(u, dw_w, dw_b, bufs, *, layer, steps, seqs):
    _, n, _, ch = bufs.shape
    groups = ch // LANES
    blk = pl.BlockSpec((seqs * steps * groups, LANES), lambda i: (i, 0))
    buf_blk = (None, seqs, CONV_HIST, ch)
    return pl.pallas_call(
        functools.partial(_conv_step_kernel, steps=steps, seqs=seqs, groups=groups),
        grid=(n // seqs,),
        in_specs=[blk, _const_spec((CONV_WIDTH, groups, LANES)), _const_spec((1, groups, LANES)),
                  pl.BlockSpec(buf_blk, lambda i: (layer, i, 0, 0))],
        out_specs=[blk, pl.BlockSpec(buf_blk, lambda i: (0, i, 0, 0))],
        out_shape=[jax.ShapeDtypeStruct(u.shape, F32), jax.ShapeDtypeStruct((1,) + bufs.shape[1:], F32)],
        scratch_shapes=[pltpu.VMEM(((steps + CONV_HIST) * groups, LANES), F32)],
        compiler_params=_params("parallel"),
        name="conv_step",
    )(u, dw_w.reshape(CONV_WIDTH, groups, LANES), dw_b.reshape(1, groups, LANES), bufs)


def _post_kernel(x_ref, oa_ref, yc_ref, sgb_ref, g1_ref, sh2_ref, sc2_ref, g2_ref, lnw_ref, lnb_ref,
                 wo_ref, nfw_ref, w1_ref, w2_ref, nlw_ref, y_ref, c_scr, h_scr, acc_scr, *, fc):
    tm, d = x_ref.shape
    groups = d // LANES
    for g in range(groups):
        c_scr[:, g * LANES:(g + 1) * LANES] = yc_ref[_group_rows(g, tm, groups), :]
    yc = c_scr[...]
    mu = jnp.mean(yc, axis=-1, keepdims=True)
    dv = yc - mu
    var = jnp.mean(dv * dv, axis=-1, keepdims=True)
    ob = _silu(dv * lax.rsqrt(var + EPS) * lnw_ref[...] + lnb_ref[...]) * sgb_ref[...].astype(F32)
    merged = jnp.dot(oa_ref[...].astype(BF16), wo_ref[0:d, :], preferred_element_type=F32)
    merged = merged + jnp.dot(ob.astype(BF16), wo_ref[d:2 * d, :], preferred_element_type=F32)
    x1 = x_ref[...] + g1_ref[...] * merged
    ms = jnp.mean(x1 * x1, axis=-1, keepdims=True)
    h2 = x1 * lax.rsqrt(ms + EPS) * nfw_ref[...]
    h2 = h2 * (1.0 + sc2_ref[...]) + sh2_ref[...]
    h_scr[...] = h2.astype(BF16)
    d_ff = w2_ref.shape[0]
    for c in range(d_ff // fc):
        gate = jnp.dot(h_scr[...], w1_ref[:, c * fc:(c + 1) * fc], preferred_element_type=F32)
        up = jnp.dot(h_scr[...], w1_ref[:, d_ff + c * fc:d_ff + (c + 1) * fc], preferred_element_type=F32)
        part = jnp.dot((_silu(gate) * up).astype(BF16), w2_ref[c * fc:(c + 1) * fc, :],
                       preferred_element_type=F32)
        if c == 0:
            acc_scr[...] = part
        else:
            acc_scr[...] += part
    x2 = x1 + g2_ref[...] * acc_scr[...]
    ms2 = jnp.mean(x2 * x2, axis=-1, keepdims=True)
    y_ref[...] = x2 * lax.rsqrt(ms2 + EPS) * nlw_ref[...]


def _post(x, oa, yc, sgb, mod, lnw, lnb, wo, nfw, w1, w2, nlw, *, tm, fc, rows_per_mod):
    m, d = x.shape
    groups = d // LANES
    mod_spec = lambda row: _mod_spec(mod, row, d, tm, rows_per_mod)
    row_spec = pl.BlockSpec((tm, d), lambda i: (i, 0))
    il_spec = pl.BlockSpec((tm * groups, LANES), lambda i: (i, 0))
    vec = _const_spec((1, d))
    return pl.pallas_call(
        functools.partial(_post_kernel, fc=fc),
        grid=(m // tm,),
        in_specs=[row_spec, row_spec, il_spec, row_spec, mod_spec(2), mod_spec(3), mod_spec(4), mod_spec(5),
                  vec, vec, _const_spec(wo.shape), vec, _const_spec(w1.shape), _const_spec(w2.shape), vec],
        out_specs=row_spec,
        out_shape=jax.ShapeDtypeStruct((m, d), F32),
        scratch_shapes=[pltpu.VMEM((tm, d), F32), pltpu.VMEM((tm, d), BF16), pltpu.VMEM((tm, d), F32)],
        compiler_params=_params("parallel"),
        name="post",
    )(x, oa, yc, sgb, mod, mod, mod, mod, lnw, lnb, wo, nfw, w1, w2, nlw)


INPROJ_TC = 256
FFN_CHUNK = 256
MOD_TN = 1024
PROMPT_TM = 512
PROMPT_CHUNK = 128
PROMPT_CHUNKS_PER_STEP = 4
PROMPT_CONV_TILE = 512
PROMPT_CONV_TB = 16
SAMPLE_SEQS = 16
SAMPLE_PAD = 16


def kernel(x_prompt, x_sample, state_hgrn, state_conv, c_prompt, c_sample, w_ada, b_ada, norm_mix_w, w_in, b_in, lb_logits, hgrn_norm_w, conv_dw_w, conv_dw_b, conv_ln_w, conv_ln_b, w_out, norm_ffn_w, w_ffn_in, w_ffn_out, norm_final_w):
    depth = w_in.shape[0]
    assert depth == 1, "single-layer configuration"
    layer = 0
    n_p, l_p, d = x_prompt.shape
    n_s, l_s, _ = x_sample.shape
    heads = d // HEAD_DIM

    w_in_b = w_in[layer].astype(BF16)
    w1 = w_ffn_in[layer].astype(BF16)
    w2 = w_ffn_out[layer].astype(BF16)
    wo = w_out[layer].astype(BF16)
    row = lambda a: a.reshape(1, -1)

    mod = _mod(jnp.concatenate([c_prompt, c_sample], axis=0), w_ada[layer], b_ada[layer], tn=MOD_TN)
    mod_p = mod[:n_p].reshape(n_p, 1, MOD_ROWS * d)
    mod_s = jnp.repeat(mod[n_p:], l_s, axis=0).reshape(1, n_s * l_s, MOD_ROWS * d)

    def inproj(x2, mod_g, tm):
        return _inproj(x2, mod_g, row(norm_mix_w[layer]), w_in_b, row(b_in[layer]), lb_logits,
                       layer=layer, tm=tm, tc=INPROJ_TC, rows_per_mod=tm if mod_g.shape[1] > 1 else l_p)

    def post(x2, oa, yc, sgb, mod_g, tm):
        return _post(x2, oa, yc, sgb, mod_g, row(conv_ln_w[layer]), row(conv_ln_b[layer]), wo,
                     row(norm_ffn_w[layer]), w1, w2, row(norm_final_w),
                     tm=tm, fc=FFN_CHUNK, rows_per_mod=tm if mod_g.shape[1] > 1 else l_p)

    xp = x_prompt.reshape(n_p * l_p, d)
    q, f, v, ga, u, sgb = inproj(xp, mod_p, PROMPT_TM)
    oa, s_p = _recur(q, f, v, ga, hgrn_norm_w[layer], n=n_p, heads=heads, chunk=PROMPT_CHUNK,
                     chunks_per_step=PROMPT_CHUNKS_PER_STEP)
    yc, b_p = _conv(u, conv_dw_w[layer], conv_dw_b[layer], n=n_p, tile=PROMPT_CONV_TILE, tb=PROMPT_CONV_TB)
    y_p = post(xp, oa, yc, sgb, mod_p, PROMPT_TM).reshape(n_p, l_p, d)

    xs = x_sample.reshape(n_s * l_s, d)
    q, f, v, ga, u, sgb = inproj(xs, mod_s, n_s * l_s)
    oa, s_s = _recur_step(q, f, v, ga, hgrn_norm_w[layer], state_hgrn, layer=layer,
                          heads=heads, steps=l_s, seqs=SAMPLE_SEQS, pad=SAMPLE_PAD)
    yc, b_s = _conv_step(u, conv_dw_w[layer], conv_dw_b[layer], state_conv, layer=layer,
                         steps=l_s, seqs=SAMPLE_SEQS)
    y_s = post(xs, oa, yc, sgb, mod_s, n_s * l_s).reshape(n_s, l_s, d)

    return (y_p, y_s, s_p, b_p, s_s, b_s)
```

```python
import functools

import numpy as np
import jax
import jax.numpy as jnp
from jax import lax
from jax.experimental import pallas as pl
from jax.experimental.pallas import tpu as pltpu

F32 = jnp.float32
BF16 = jnp.bfloat16

EPS = 1e-6
LANES = 128
SUBLANES = 8
HEAD_DIM = LANES
CONV_WIDTH = 31
CONV_HIST = CONV_WIDTH - 1
VMEM_LIMIT_BYTES = 56 * 1024 * 1024

IN_SEGMENTS = 8
MOD_ROWS = 6

NT = (((1,), (1,)), ((), ()))
TN = (((0,), (0,)), ((), ()))


def _silu(x):
    return x * jax.nn.sigmoid(x)


def _params(*sem):
    return pltpu.CompilerParams(dimension_semantics=sem, vmem_limit_bytes=VMEM_LIMIT_BYTES)


def _const_spec(shape):
    nd = len(shape)
    return pl.BlockSpec(shape, lambda *_: (0,) * nd, pipeline_mode=pl.Buffered(1))


def _mod_rows(ref, tm):
    m = ref[...]
    r = m.shape[0]
    return m if r in (1, tm) else jnp.tile(m, (tm // r, 1))


def _group_rows(g, n, groups):
    return pl.ds(g, n, stride=groups)


def _mod_kernel(c_ref, w_ref, b_ref, o_ref):
    a = _silu(c_ref[...]).astype(BF16)
    o_ref[...] = jnp.dot(a, w_ref[...].astype(BF16), preferred_element_type=F32) + b_ref[...]


def _mod(c, w_ada, b_ada, *, tn=512):
    n, d = c.shape
    cols = w_ada.shape[1]
    return pl.pallas_call(
        _mod_kernel,
        grid=(cols // tn,),
        in_specs=[pl.BlockSpec((n, d), lambda j: (0, 0)),
                  pl.BlockSpec((d, tn), lambda j: (0, j)),
                  pl.BlockSpec((1, tn), lambda j: (0, j))],
        out_specs=pl.BlockSpec((n, tn), lambda j: (0, j)),
        out_shape=jax.ShapeDtypeStruct((n, cols), F32),
        compiler_params=_params("parallel"),
        name="mod",
    )(c, w_ada, b_ada.reshape(1, cols))


def _inproj_kernel(x_ref, sh_ref, sc_ref, nw_ref, w_ref, b_ref, lbl_ref,
                   q_ref, f_ref, v_ref, ga_ref, u_ref, sgb_ref, h_scr, *, layer, tc):
    tm, d = x_ref.shape
    groups = d // LANES
    x = x_ref[...]
    ms = jnp.mean(x * x, axis=-1, keepdims=True)
    h = x * lax.rsqrt(ms + EPS) * nw_ref[...]
    h = h * (1.0 + _mod_rows(sc_ref, tm)) + _mod_rows(sh_ref, tm)
    h_scr[...] = h.astype(BF16)
    for j in range(d // tc):
        def seg(s):
            cols = slice(s * d + j * tc, s * d + (j + 1) * tc)
            return jnp.dot(h_scr[...], w_ref[:, cols], preferred_element_type=F32) + b_ref[:, cols]

        def put(ref, val):
            for g in range(tc // LANES):
                rows = _group_rows(j * (tc // LANES) + g, tm, groups)
                ref[rows, :] = val[:, g * LANES:(g + 1) * LANES]

        put(q_ref, _silu(seg(0)))
        lg = lbl_ref[:, j * tc:(j + 1) * tc]
        e = jnp.exp(lg - jnp.max(lg, axis=0, keepdims=True))
        lb = jnp.sum(e[:layer + 1], axis=0, keepdims=True) / jnp.sum(e, axis=0, keepdims=True)
        put(f_ref, lb + (1.0 - lb) * jax.nn.sigmoid(seg(1)))
        put(v_ref, seg(2))
        og_act = _silu(seg(3))
        ga_ref[:, j * tc:(j + 1) * tc] = og_act * jax.nn.sigmoid(seg(6))
        glu_a = seg(4)
        put(u_ref, glu_a * jax.nn.sigmoid(seg(5)))
        sgb_ref[:, j * tc:(j + 1) * tc] = jax.nn.sigmoid(seg(7)).astype(sgb_ref.dtype)


def _mod_spec(mod, row, d, tm, rows_per_mod):
    tiles_per_mod = rows_per_mod // tm
    return pl.BlockSpec((None, mod.shape[1], d), lambda i: (i // tiles_per_mod, 0, row))


def _inproj(x, mod, nw, w, b, lbl, *, layer, tm, tc, rows_per_mod):
    m, d = x.shape
    groups = d // LANES
    mod_spec = lambda row: _mod_spec(mod, row, d, tm, rows_per_mod)
    row_spec = pl.BlockSpec((tm, d), lambda i: (i, 0))
    il_spec = pl.BlockSpec((tm * groups, LANES), lambda i: (i, 0))
    il_shape = jax.ShapeDtypeStruct((m * groups, LANES), F32)
    return pl.pallas_call(
        functools.partial(_inproj_kernel, layer=layer, tc=tc),
        grid=(m // tm,),
        in_specs=[row_spec, mod_spec(0), mod_spec(1),
                  _const_spec((1, d)), _const_spec(w.shape), _const_spec(b.shape),
                  _const_spec(lbl.shape)],
        out_specs=[il_spec, il_spec, il_spec, row_spec, il_spec, row_spec],
        out_shape=[il_shape, il_shape, il_shape, jax.ShapeDtypeStruct((m, d), F32), il_shape,
                   jax.ShapeDtypeStruct((m, d), BF16)],
        scratch_shapes=[pltpu.VMEM((tm, d), BF16)],
        compiler_params=_params("parallel"),
        name="inproj",
    )(x, mod, mod, nw, w, b, lbl)


def _split_level_table(c):
    t = np.arange(c)[:, None]
    s = np.arange(c)[None, :]
    x = t ^ s
    lvl = np.zeros((c, c), np.int32)
    for l in range(1, c.bit_length()):
        lvl[(x >> (l - 1)) == 1] = l
    return np.where(t > s, lvl, 0).astype(np.int32)


def _recur_kernel(q_ref, f_ref, v_ref, ga_ref, nw_ref, lvl_ref, o_ref, s_out_ref,
                  st_scr, x_scr, d_scr, *, heads, chunk):
    c_idx = pl.program_id(1)
    nlev = chunk.bit_length() - 1

    @pl.when(c_idx == 0)
    def _():
        st_scr[...] = jnp.zeros_like(st_scr)

    def to3(a):
        return a.reshape(chunk, heads, HEAD_DIM)

    def flat(a):
        return a.reshape(chunk * heads, HEAD_DIM)

    lvl = lvl_ref[...]
    rows_chunk = chunk * heads

    def chunk_body(ci, carry):
        base = pl.multiple_of(ci * rows_chunk, rows_chunk)
        whole = pl.ds(base, rows_chunk)
        steps = pl.ds(pl.multiple_of(ci * chunk, chunk), chunk)
        f3 = to3(f_ref[whole, :])
        q3 = to3(q_ref[whole, :])
        v3 = to3(v_ref[whole, :])
        k3 = 1.0 - f3
        p, r = f3, jnp.ones_like(f3)
        half = 1
        for l in range(nlev):
            blk = 2 * half
            shp = (chunk // blk, blk, heads, HEAD_DIM)
            p4, r4, q4, k4 = (a.reshape(shp) for a in (p, r, q3, k3))
            x = jnp.concatenate([k4[:, :half] * r4[:, :half], q4[:, half:] * p4[:, half:]], axis=1)
            x_scr[l] = flat(x)
            left_tot = p4[:, half - 1:half]
            right_tot = p4[:, blk - 1:blk]
            p = jnp.concatenate([p4[:, :half], p4[:, half:] * left_tot], axis=1).reshape(f3.shape)
            r = jnp.concatenate([r4[:, :half] * right_tot, r4[:, half:]], axis=1).reshape(f3.shape)
            half = blk
        x_scr[nlev] = flat(q3 * p)
        x_scr[nlev + 1] = flat(k3 * r)
        d_scr[...] = flat(jnp.sum(q3 * k3, axis=-1, keepdims=True) * v3)
        p_last = p[chunk - 1]

        for h in range(heads):
            rows = _group_rows(h, chunk, heads)
            blk_rows = _group_rows(base + h, chunk, heads)
            att = jnp.zeros((chunk, chunk), F32)
            for l in range(nlev):
                xb = x_scr[l, rows, :].astype(BF16)
                a = lax.dot_general(xb, xb, NT, preferred_element_type=F32)
                att = jnp.where(lvl == l + 1, a, att)
            qt = x_scr[nlev, rows, :].astype(BF16)
            kt = x_scr[nlev + 1, rows, :].astype(BF16)
            vb = v_ref[blk_rows, :].astype(BF16)
            st0 = st_scr[h]
            o = lax.dot_general(qt, st0.astype(BF16), NT, preferred_element_type=F32)
            o = o + jnp.dot(att.astype(BF16), vb, preferred_element_type=F32) + d_scr[rows, :]
            st_scr[h] = st0 * p_last[h:h + 1, :] + lax.dot_general(vb, kt, TN,
                                                                   preferred_element_type=F32)
            o = o * lax.rsqrt(jnp.mean(o * o, axis=-1, keepdims=True) + EPS)
            cols = slice(h * HEAD_DIM, (h + 1) * HEAD_DIM)
            o_ref[steps, cols] = o * nw_ref[h:h + 1, :] * ga_ref[steps, cols]
        return carry

    lax.fori_loop(0, q_ref.shape[0] // rows_chunk, chunk_body, 0)

    @pl.when(c_idx == pl.num_programs(1) - 1)
    def _():
        for h in range(heads):
            s_out_ref[h] = st_scr[h].T


def _recur(q, f, v, ga, norm_w, *, n, heads, chunk, chunks_per_step):
    rows_total = q.shape[0]
    l = rows_total // (n * heads)
    nlev = chunk.bit_length() - 1
    rows = chunk * heads
    steps = l // (chunk * chunks_per_step)
    blk = pl.BlockSpec((chunks_per_step * rows, HEAD_DIM), lambda i, c: (i * steps + c, 0))
    row_blk = pl.BlockSpec((chunks_per_step * chunk, heads * HEAD_DIM), lambda i, c: (i * steps + c, 0))
    st_spec = pl.BlockSpec((None, None, heads, HEAD_DIM, HEAD_DIM), lambda i, c: (0, i, 0, 0, 0))
    lvl = jnp.asarray(_split_level_table(chunk))
    return pl.pallas_call(
        functools.partial(_recur_kernel, heads=heads, chunk=chunk),
        grid=(n, steps),
        in_specs=[blk, blk, blk, row_blk, _const_spec((heads, HEAD_DIM)), _const_spec((chunk, chunk))],
        out_specs=[row_blk, st_spec],
        out_shape=[jax.ShapeDtypeStruct(ga.shape, F32),
                   jax.ShapeDtypeStruct((1, n, heads, HEAD_DIM, HEAD_DIM), F32)],
        scratch_shapes=[pltpu.VMEM((heads, HEAD_DIM, HEAD_DIM), F32),
                        pltpu.VMEM((nlev + 2, rows, HEAD_DIM), F32),
                        pltpu.VMEM((rows, HEAD_DIM), F32)],
        compiler_params=_params("parallel", "arbitrary"),
        name="recur",
    )(q, f, v, ga, norm_w.reshape(heads, HEAD_DIM), lvl)


def _recur_step_kernel(q_ref, f_ref, v_ref, ga_ref, nw_ref, s0_ref, o_ref, s_out_ref,
                       qt_scr, kt_scr, vp_scr, oi_scr, pt_scr, g_scr, *, heads, steps, seqs, pad):
    rows_seq = steps * heads
    zero = jnp.zeros((pad * heads, HEAD_DIM), F32)
    qt_scr[...] = zero
    kt_scr[...] = zero
    vp_scr[...] = zero
    pt_scr[...] = jnp.zeros_like(pt_scr)

    def regroup(t, h):
        return pl.ds(t * heads + h, seqs, stride=rows_seq)

    for t in range(steps):
        for h in range(heads):
            g_scr[regroup(t, h), :] = ga_ref[t, :, h * HEAD_DIM:(h + 1) * HEAD_DIM]

    def seq_body(g, carry):
        base = pl.multiple_of(g * rows_seq, rows_seq)
        blk = pl.ds(base, rows_seq)
        seq = pl.ds(pl.multiple_of(g * heads, heads), heads)
        to3 = lambda a: a.reshape(steps, heads, HEAD_DIM)
        q3, f3, v3 = (jnp.stack([ref[t, seq, :] for t in range(steps)]) for ref in (q_ref, f_ref, v_ref))
        k3 = 1.0 - f3
        p = [f3[0]]
        for t in range(1, steps):
            p.append(p[-1] * f3[t])
        r = [None] * steps
        r[steps - 1] = jnp.ones_like(f3[0])
        for s in range(steps - 2, -1, -1):
            r[s] = r[s + 1] * f3[s + 1]
        intra = []
        for t in range(steps):
            acc = jnp.sum(q3[t] * k3[t], axis=-1, keepdims=True) * v3[t]
            dec = None
            for s in range(t - 1, -1, -1):
                dec = f3[s + 1] if dec is None else dec * f3[s + 1]
                acc = acc + jnp.sum(q3[t] * k3[s] * dec, axis=-1, keepdims=True) * v3[s]
            intra.append(acc)
        for t in range(steps):
            rows_t = pl.ds(t * heads, heads)
            qt_scr[rows_t, :] = q3[t] * p[t]
            kt_scr[rows_t, :] = k3[t] * r[t]
            vp_scr[rows_t, :] = v3[t]
        pt_scr[0:heads, :] = p[steps - 1]
        p_col = pt_scr[...].T
        for h in range(heads):
            rows = _group_rows(h, pad, heads)
            qt = qt_scr[rows, :].astype(BF16)
            kt = kt_scr[rows, :].astype(BF16)
            vb = vp_scr[rows, :].astype(BF16)
            s0 = s0_ref[g, h]
            oi_scr[rows, :] = jnp.dot(qt, s0.astype(BF16), preferred_element_type=F32)
            s_out_ref[g, h] = s0 * p_col[:, h:h + 1] + lax.dot_general(
                kt, vb, TN, preferred_element_type=F32)
        o3 = to3(oi_scr[0:rows_seq, :]) + jnp.stack(intra, axis=0)
        o3 = o3 * lax.rsqrt(jnp.mean(o3 * o3, axis=-1, keepdims=True) + EPS)
        g_scr[blk, :] = (o3 * nw_ref[...] * to3(g_scr[blk, :])).reshape(rows_seq, HEAD_DIM)
        return carry

    lax.fori_loop(0, seqs, seq_body, 0)
    for t in range(steps):
        for h in range(heads):
            o_ref[t, :, h * HEAD_DIM:(h + 1) * HEAD_DIM] = g_scr[regroup(t, h), :]


def _recur_step(q, f, v, ga, norm_w, states, *, layer, heads, steps, seqs, pad):
    n = states.shape[1]
    rows = seqs * steps * heads
    q, f, v = (a.reshape(steps, n * heads, HEAD_DIM) for a in (q, f, v))
    ga = ga.reshape(steps, n, heads * HEAD_DIM)
    blk = pl.BlockSpec((steps, seqs * heads, HEAD_DIM), lambda i: (0, i, 0))
    row_blk = pl.BlockSpec((steps, seqs, heads * HEAD_DIM), lambda i: (0, i, 0))
    st_blk = (None, seqs, heads, HEAD_DIM, HEAD_DIM)
    scr = pltpu.VMEM((pad * heads, HEAD_DIM), F32)
    o, s_new = pl.pallas_call(
        functools.partial(_recur_step_kernel, heads=heads, steps=steps, seqs=seqs, pad=pad),
        grid=(n // seqs,),
        in_specs=[blk, blk, blk, row_blk, _const_spec((heads, HEAD_DIM)),
                  pl.BlockSpec(st_blk, lambda i: (layer, i, 0, 0, 0))],
        out_specs=[row_blk, pl.BlockSpec(st_blk, lambda i: (0, i, 0, 0, 0))],
        out_shape=[jax.ShapeDtypeStruct(ga.shape, F32), jax.ShapeDtypeStruct((1,) + states.shape[1:], F32)],
        scratch_shapes=[scr, scr, scr, scr, pltpu.VMEM((HEAD_DIM, HEAD_DIM), F32),
                        pltpu.VMEM((rows, HEAD_DIM), F32)],
        compiler_params=_params("parallel"),
        name="recur_step",
    )(q, f, v, ga, norm_w.reshape(heads, HEAD_DIM), states)
    return o.reshape(steps * n, heads * HEAD_DIM), s_new


def _conv_taps(ext_scr, w_ref, b_ref, y_ref, *, ext_row0, out_row0, tb, groups):
    acc = [jnp.broadcast_to(b_ref[...], (tb, groups, LANES)), jnp.zeros((tb, groups, LANES), F32)]
    for j in range(CONV_WIDTH):
        start = (ext_row0 + j) * groups
        if not isinstance(start, int):
            start = pl.multiple_of(start, groups)
        e = ext_scr[pl.ds(start, tb * groups), :].reshape(tb, groups, LANES)
        acc[j % 2] = acc[j % 2] + w_ref[j] * e
    y_ref[pl.ds(out_row0 * groups, tb * groups), :] = (acc[0] + acc[1]).reshape(tb * groups, LANES)


def _conv_kernel(u_ref, w_ref, b_ref, y_ref, buf_out_ref, ext_scr, *, tile, tb, groups):
    t_idx = pl.program_id(1)
    hist = CONV_HIST * groups

    @pl.when(t_idx == 0)
    def _():
        ext_scr[0:hist, :] = jnp.zeros((hist, LANES), F32)

    @pl.when(t_idx > 0)
    def _():
        ext_scr[0:hist, :] = ext_scr[tile * groups:tile * groups + hist, :]

    ext_scr[hist:hist + tile * groups, :] = u_ref[...]

    def block_body(i, carry):
        r0 = pl.multiple_of(i * tb, tb)
        _conv_taps(ext_scr, w_ref, b_ref, y_ref, ext_row0=r0, out_row0=r0, tb=tb, groups=groups)
        return carry

    lax.fori_loop(0, tile // tb, block_body, 0)

    @pl.when(t_idx == pl.num_programs(1) - 1)
    def _():
        for g in range(groups):
            buf_out_ref[:, g * LANES:(g + 1) * LANES] = ext_scr[
                pl.ds(tile * groups + g, CONV_HIST, stride=groups), :]


def _conv(u, dw_w, dw_b, *, n, tile, tb):
    ch = dw_w.shape[1]
    groups = ch // LANES
    l = u.shape[0] // (n * groups)
    steps = l // tile
    blk = pl.BlockSpec((tile * groups, LANES), lambda i, t: (i * steps + t, 0))
    return pl.pallas_call(
        functools.partial(_conv_kernel, tile=tile, tb=tb, groups=groups),
        grid=(n, steps),
        in_specs=[blk, _const_spec((CONV_WIDTH, groups, LANES)), _const_spec((1, groups, LANES))],
        out_specs=[blk, pl.BlockSpec((None, None, CONV_HIST, ch), lambda i, t: (0, i, 0, 0))],
        out_shape=[jax.ShapeDtypeStruct(u.shape, F32), jax.ShapeDtypeStruct((1, n, CONV_HIST, ch), F32)],
        scratch_shapes=[pltpu.VMEM(((tile + CONV_HIST) * groups, LANES), F32)],
        compiler_params=_params("parallel", "arbitrary"),
        name="conv",
    )(u, dw_w.reshape(CONV_WIDTH, groups, LANES), dw_b.reshape(1, groups, LANES))


def _conv_step_kernel(u_ref, w_ref, b_ref, buf_ref, y_ref, buf_out_ref, *, steps, seqs, groups):
    for g in range(groups):
        cols = slice(g * LANES, (g + 1) * LANES)
        rows = _group_rows(g, seqs, groups)
        ext = [buf_ref[t, :, cols] for t in range(CONV_HIST)] + [u_ref[t, rows, :] for t in range(steps)]
        for t in range(steps):
            acc = [jnp.broadcast_to(b_ref[0, g:g + 1, :], (seqs, LANES)), jnp.zeros((seqs, LANES), F32)]
            for j in range(CONV_WIDTH):
                acc[j % 2] = acc[j % 2] + w_ref[j, g:g + 1, :] * ext[t + j]
            y_ref[t, rows, :] = acc[0] + acc[1]
        for t in range(CONV_HIST):
            buf_out_ref[t, :, cols] = ext[t + steps]


def _conv_step(u, dw_w, dw_b, bufs, *, layer, steps, seqs):
    _, _, n, ch = bufs.shape
    groups = ch // LANES
    blk = pl.BlockSpec((steps, seqs * groups, LANES), lambda i: (0, i, 0))
    buf_blk = (None, CONV_HIST, seqs, ch)
    y, buf_new = pl.pallas_call(
        functools.partial(_conv_step_kernel, steps=steps, seqs=seqs, groups=groups),
        grid=(n // seqs,),
        in_specs=[blk, _const_spec((CONV_WIDTH, groups, LANES)), _const_spec((1, groups, LANES)),
                  pl.BlockSpec(buf_blk, lambda i: (layer, 0, i, 0))],
        out_specs=[blk, pl.BlockSpec(buf_blk, lambda i: (0, 0, i, 0))],
        out_shape=[jax.ShapeDtypeStruct((steps, n * groups, LANES), F32),
                   jax.ShapeDtypeStruct((1,) + bufs.shape[1:], F32)],
        compiler_params=_params("parallel"),
        name="conv_step",
    )(u.reshape(steps, n * groups, LANES), dw_w.reshape(CONV_WIDTH, groups, LANES),
      dw_b.reshape(1, groups, LANES), bufs)
    return y.reshape(u.shape), buf_new


def _post_kernel(x_ref, oa_ref, yc_ref, sgb_ref, g1_ref, sh2_ref, sc2_ref, g2_ref, lnw_ref, lnb_ref,
                 wo_ref, nfw_ref, w1_ref, w2_ref, nlw_ref, y_ref, c_scr, h_scr, acc_scr, *, fc):
    tm, d = x_ref.shape
    groups = d // LANES
    for g in range(groups):
        c_scr[:, g * LANES:(g + 1) * LANES] = yc_ref[_group_rows(g, tm, groups), :]
    yc = c_scr[...]
    mu = jnp.mean(yc, axis=-1, keepdims=True)
    dv = yc - mu
    var = jnp.mean(dv * dv, axis=-1, keepdims=True)
    ob = _silu(dv * lax.rsqrt(var + EPS) * lnw_ref[...] + lnb_ref[...]) * sgb_ref[...].astype(F32)
    merged = jnp.dot(oa_ref[...].astype(BF16), wo_ref[0:d, :], preferred_element_type=F32)
    merged = merged + jnp.dot(ob.astype(BF16), wo_ref[d:2 * d, :], preferred_element_type=F32)
    x1 = x_ref[...] + _mod_rows(g1_ref, tm) * merged
    ms = jnp.mean(x1 * x1, axis=-1, keepdims=True)
    h2 = x1 * lax.rsqrt(ms + EPS) * nfw_ref[...]
    h2 = h2 * (1.0 + _mod_rows(sc2_ref, tm)) + _mod_rows(sh2_ref, tm)
    h_scr[...] = h2.astype(BF16)
    d_ff = w2_ref.shape[0]
    for c in range(d_ff // fc):
        gate = jnp.dot(h_scr[...], w1_ref[:, c * fc:(c + 1) * fc], preferred_element_type=F32)
        up = jnp.dot(h_scr[...], w1_ref[:, d_ff + c * fc:d_ff + (c + 1) * fc], preferred_element_type=F32)
        part = jnp.dot((_silu(gate) * up).astype(BF16), w2_ref[c * fc:(c + 1) * fc, :],
                       preferred_element_type=F32)
        if c == 0:
            acc_scr[...] = part
        else:
            acc_scr[...] += part
    x2 = x1 + _mod_rows(g2_ref, tm) * acc_scr[...]
    ms2 = jnp.mean(x2 * x2, axis=-1, keepdims=True)
    y_ref[...] = x2 * lax.rsqrt(ms2 + EPS) * nlw_ref[...]


def _post(x, oa, yc, sgb, mod, lnw, lnb, wo, nfw, w1, w2, nlw, *, tm, fc, rows_per_mod):
    m, d = x.shape
    groups = d // LANES
    mod_spec = lambda row: _mod_spec(mod, row, d, tm, rows_per_mod)
    row_spec = pl.BlockSpec((tm, d), lambda i: (i, 0))
    il_spec = pl.BlockSpec((tm * groups, LANES), lambda i: (i, 0))
    vec = _const_spec((1, d))
    return pl.pallas_call(
        functools.partial(_post_kernel, fc=fc),
        grid=(m // tm,),
        in_specs=[row_spec, row_spec, il_spec, row_spec, mod_spec(2), mod_spec(3), mod_spec(4), mod_spec(5),
                  vec, vec, _const_spec(wo.shape), vec, _const_spec(w1.shape), _const_spec(w2.shape), vec],
        out_specs=row_spec,
        out_shape=jax.ShapeDtypeStruct((m, d), F32),
        scratch_shapes=[pltpu.VMEM((tm, d), F32), pltpu.VMEM((tm, d), BF16), pltpu.VMEM((tm, d), F32)],
        compiler_params=_params("parallel"),
        name="post",
    )(x, oa, yc, sgb, mod, mod, mod, mod, lnw, lnb, wo, nfw, w1, w2, nlw)


INPROJ_TC = 256
FFN_CHUNK = 256
MOD_TN = 1024
PROMPT_TM = 512
PROMPT_CHUNK = 128
PROMPT_CHUNKS_PER_STEP = 4
PROMPT_CONV_TILE = 512
PROMPT_CONV_TB = 16
SAMPLE_SEQS = 16
SAMPLE_PAD = 16


def kernel(x_prompt, x_sample, state_hgrn, state_conv, c_prompt, c_sample, w_ada, b_ada, norm_mix_w, w_in, b_in, lb_logits, hgrn_norm_w, conv_dw_w, conv_dw_b, conv_ln_w, conv_ln_b, w_out, norm_ffn_w, w_ffn_in, w_ffn_out, norm_final_w):
    depth = w_in.shape[0]
    assert depth == 1, "single-layer configuration"
    layer = 0
    n_p, l_p, d = x_prompt.shape
    n_s, l_s, _ = x_sample.shape
    heads = d // HEAD_DIM

    w_in_b = w_in[layer].astype(BF16)
    w1 = w_ffn_in[layer].astype(BF16)
    w2 = w_ffn_out[layer].astype(BF16)
    wo = w_out[layer].astype(BF16)
    row = lambda a: a.reshape(1, -1)

    mod = _mod(jnp.concatenate([c_prompt, c_sample], axis=0), w_ada[layer], b_ada[layer], tn=MOD_TN)
    mod_p = mod[:n_p].reshape(n_p, 1, MOD_ROWS * d)
    mod_s = mod[n_p:].reshape(1, n_s, MOD_ROWS * d)

    def inproj(x2, mod_g, tm):
        return _inproj(x2, mod_g, row(norm_mix_w[layer]), w_in_b, row(b_in[layer]), lb_logits,
                       layer=layer, tm=tm, tc=INPROJ_TC, rows_per_mod=tm if mod_g.shape[1] > 1 else l_p)

    def post(x2, oa, yc, sgb, mod_g, tm):
        return _post(x2, oa, yc, sgb, mod_g, row(conv_ln_w[layer]), row(conv_ln_b[layer]), wo,
                     row(norm_ffn_w[layer]), w1, w2, row(norm_final_w),
                     tm=tm, fc=FFN_CHUNK, rows_per_mod=tm if mod_g.shape[1] > 1 else l_p)

    xp = x_prompt.reshape(n_p * l_p, d)
    q, f, v, ga, u, sgb = inproj(xp, mod_p, PROMPT_TM)
    oa, s_p = _recur(q, f, v, ga, hgrn_norm_w[layer], n=n_p, heads=heads, chunk=PROMPT_CHUNK,
                     chunks_per_step=PROMPT_CHUNKS_PER_STEP)
    yc, b_p = _conv(u, conv_dw_w[layer], conv_dw_b[layer], n=n_p, tile=PROMPT_CONV_TILE, tb=PROMPT_CONV_TB)
    y_p = post(xp, oa, yc, sgb, mod_p, PROMPT_TM).reshape(n_p, l_p, d)

    xs = x_sample.transpose(1, 0, 2).reshape(l_s * n_s, d)
    q, f, v, ga, u, sgb = inproj(xs, mod_s, n_s * l_s)
    oa, s_s = _recur_step(q, f, v, ga, hgrn_norm_w[layer], state_hgrn, layer=layer,
                          heads=heads, steps=l_s, seqs=SAMPLE_SEQS, pad=SAMPLE_PAD)
    yc, b_s = _conv_step(u, conv_dw_w[layer], conv_dw_b[layer], state_conv.transpose(0, 2, 1, 3),
                         layer=layer, steps=l_s, seqs=SAMPLE_SEQS)
    y_s = post(xs, oa, yc, sgb, mod_s, n_s * l_s).reshape(l_s, n_s, d).transpose(1, 0, 2)

    return (y_p, y_s, s_p, b_p, s_s, b_s.transpose(0, 2, 1, 3))
```

```python
import functools

import numpy as np
import jax
import jax.numpy as jnp
from jax import lax
from jax.experimental import pallas as pl
from jax.experimental.pallas import tpu as pltpu

F32 = jnp.float32
BF16 = jnp.bfloat16

EPS = 1e-6
LANES = 128
SUBLANES = 8
HEAD_DIM = LANES
CONV_WIDTH = 31
CONV_HIST = CONV_WIDTH - 1
VMEM_LIMIT_BYTES = 56 * 1024 * 1024

IN_SEGMENTS = 8
MOD_ROWS = 6

NT = (((1,), (1,)), ((), ()))
TN = (((0,), (0,)), ((), ()))


def _silu(x):
    return x * jax.nn.sigmoid(x)


def _params(*sem):
    return pltpu.CompilerParams(dimension_semantics=sem, vmem_limit_bytes=VMEM_LIMIT_BYTES)


def _const_spec(shape):
    nd = len(shape)
    return pl.BlockSpec(shape, lambda *_: (0,) * nd, pipeline_mode=pl.Buffered(1))


def _mod_rows(ref, tm):
    m = ref[...]
    r = m.shape[0]
    return m if r in (1, tm) else jnp.tile(m, (tm // r, 1))


def _group_rows(g, n, groups):
    return pl.ds(g, n, stride=groups)


def _mod_kernel(c_ref, w_ref, b_ref, o_ref):
    a = _silu(c_ref[...]).astype(BF16)
    o_ref[...] = jnp.dot(a, w_ref[...].astype(BF16), preferred_element_type=F32) + b_ref[...]


def _mod(c, w_ada, b_ada, *, tn=512):
    n, d = c.shape
    cols = w_ada.shape[1]
    return pl.pallas_call(
        _mod_kernel,
        grid=(cols // tn,),
        in_specs=[pl.BlockSpec((n, d), lambda j: (0, 0)),
                  pl.BlockSpec((d, tn), lambda j: (0, j)),
                  pl.BlockSpec((1, tn), lambda j: (0, j))],
        out_specs=pl.BlockSpec((n, tn), lambda j: (0, j)),
        out_shape=jax.ShapeDtypeStruct((n, cols), F32),
        compiler_params=_params("parallel"),
        name="mod",
    )(c, w_ada, b_ada.reshape(1, cols))


def _inproj_kernel(x_ref, sh_ref, sc_ref, nw_ref, w_ref, b_ref, lbl_ref,
                   q_ref, f_ref, v_ref, ga_ref, u_ref, sgb_ref, h_scr, *, layer, tc):
    tm, d = x_ref.shape
    groups = d // LANES
    x = x_ref[...]
    ms = jnp.mean(x * x, axis=-1, keepdims=True)
    h = x * lax.rsqrt(ms + EPS) * nw_ref[...]
    h = h * (1.0 + _mod_rows(sc_ref, tm)) + _mod_rows(sh_ref, tm)
    h_scr[...] = h.astype(BF16)
    for j in range(d // tc):
        def seg(s):
            cols = slice(s * d + j * tc, s * d + (j + 1) * tc)
            return jnp.dot(h_scr[...], w_ref[:, cols], preferred_element_type=F32) + b_ref[:, cols]

        def put(ref, val):
            for g in range(tc // LANES):
                rows = _group_rows(j * (tc // LANES) + g, tm, groups)
                ref[rows, :] = val[:, g * LANES:(g + 1) * LANES]

        put(q_ref, _silu(seg(0)))
        lg = lbl_ref[:, j * tc:(j + 1) * tc]
        e = jnp.exp(lg - jnp.max(lg, axis=0, keepdims=True))
        lb = jnp.sum(e[:layer + 1], axis=0, keepdims=True) / jnp.sum(e, axis=0, keepdims=True)
        put(f_ref, lb + (1.0 - lb) * jax.nn.sigmoid(seg(1)))
        put(v_ref, seg(2))
        og_act = _silu(seg(3))
        ga_ref[:, j * tc:(j + 1) * tc] = og_act * jax.nn.sigmoid(seg(6))
        glu_a = seg(4)
        put(u_ref, glu_a * jax.nn.sigmoid(seg(5)))
        sgb_ref[:, j * tc:(j + 1) * tc] = jax.nn.sigmoid(seg(7)).astype(sgb_ref.dtype)


def _mod_spec(mod, row, d, tm, rows_per_mod):
    tiles_per_mod = rows_per_mod // tm
    return pl.BlockSpec((None, mod.shape[1], d), lambda i: (i // tiles_per_mod, 0, row))


def _inproj(x, mod, nw, w, b, lbl, *, layer, tm, tc, rows_per_mod):
    m, d = x.shape
    groups = d // LANES
    mod_spec = lambda row: _mod_spec(mod, row, d, tm, rows_per_mod)
    row_spec = pl.BlockSpec((tm, d), lambda i: (i, 0))
    il_spec = pl.BlockSpec((tm * groups, LANES), lambda i: (i, 0))
    il_shape = jax.ShapeDtypeStruct((m * groups, LANES), F32)
    return pl.pallas_call(
        functools.partial(_inproj_kernel, layer=layer, tc=tc),
        grid=(m // tm,),
        in_specs=[row_spec, mod_spec(0), mod_spec(1),
                  _const_spec((1, d)), _const_spec(w.shape), _const_spec(b.shape),
                  _const_spec(lbl.shape)],
        out_specs=[il_spec, il_spec, il_spec, row_spec, il_spec, row_spec],
        out_shape=[il_shape, il_shape, il_shape, jax.ShapeDtypeStruct((m, d), F32), il_shape,
                   jax.ShapeDtypeStruct((m, d), BF16)],
        scratch_shapes=[pltpu.VMEM((tm, d), BF16)],
        compiler_params=_params("parallel"),
        name="inproj",
    )(x, mod, mod, nw, w, b, lbl)


def _split_level_table(c):
    t = np.arange(c)[:, None]
    s = np.arange(c)[None, :]
    x = t ^ s
    lvl = np.zeros((c, c), np.int32)
    for l in range(1, c.bit_length()):
        lvl[(x >> (l - 1)) == 1] = l
    return np.where(t > s, lvl, 0).astype(np.int32)


def _recur_kernel(q_ref, f_ref, v_ref, ga_ref, nw_ref, lvl_ref, o_ref, s_out_ref,
                  st_scr, x_scr, d_scr, *, heads, chunk):
    c_idx = pl.program_id(1)
    nlev = chunk.bit_length() - 1

    @pl.when(c_idx == 0)
    def _():
        st_scr[...] = jnp.zeros_like(st_scr)

    def to3(a):
        return a.reshape(chunk, heads, HEAD_DIM)

    def flat(a):
        return a.reshape(chunk * heads, HEAD_DIM)

    lvl = lvl_ref[...]
    rows_chunk = chunk * heads

    def chunk_body(ci, carry):
        base = pl.multiple_of(ci * rows_chunk, rows_chunk)
        whole = pl.ds(base, rows_chunk)
        steps = pl.ds(pl.multiple_of(ci * chunk, chunk), chunk)
        f3 = to3(f_ref[whole, :])
        q3 = to3(q_ref[whole, :])
        v3 = to3(v_ref[whole, :])
        k3 = 1.0 - f3
        d_scr[...] = flat(jnp.sum(q3 * k3, axis=-1, keepdims=True) * v3)
        qp, kr, tot = q3 * f3, k3, f3
        half = 1
        for l in range(nlev):
            blk = 2 * half
            shp = (chunk // blk, blk, heads, HEAD_DIM)
            qp4, kr4 = qp.reshape(shp), kr.reshape(shp)
            x_scr[l] = flat(jnp.concatenate([kr4[:, :half], qp4[:, half:]], axis=1))
            tot2 = tot.reshape(chunk // blk, 2, heads, HEAD_DIM)
            left_tot, right_tot = tot2[:, 0:1], tot2[:, 1:2]
            qp = jnp.concatenate([qp4[:, :half], qp4[:, half:] * left_tot], axis=1).reshape(f3.shape)
            kr = jnp.concatenate([kr4[:, :half] * right_tot, kr4[:, half:]], axis=1).reshape(f3.shape)
            tot = (left_tot * right_tot).reshape(chunk // blk, heads, HEAD_DIM)
            half = blk
        x_scr[nlev] = flat(qp)
        x_scr[nlev + 1] = flat(kr)
        p_last = tot[0]

        for h in range(heads):
            rows = _group_rows(h, chunk, heads)
            blk_rows = _group_rows(base + h, chunk, heads)
            att = jnp.zeros((chunk, chunk), F32)
            for l in range(nlev):
                xb = x_scr[l, rows, :].astype(BF16)
                a = lax.dot_general(xb, xb, NT, preferred_element_type=F32)
                att = jnp.where(lvl == l + 1, a, att)
            qt = x_scr[nlev, rows, :].astype(BF16)
            kt = x_scr[nlev + 1, rows, :].astype(BF16)
            vb = v_ref[blk_rows, :].astype(BF16)
            st0 = st_scr[h]
            o = lax.dot_general(qt, st0.astype(BF16), NT, preferred_element_type=F32)
            o = o + jnp.dot(att.astype(BF16), vb, preferred_element_type=F32) + d_scr[rows, :]
            st_scr[h] = st0 * p_last[h:h + 1, :] + lax.dot_general(vb, kt, TN,
                                                                   preferred_element_type=F32)
            o = o * lax.rsqrt(jnp.mean(o * o, axis=-1, keepdims=True) + EPS)
            cols = slice(h * HEAD_DIM, (h + 1) * HEAD_DIM)
            o_ref[steps, cols] = (o * nw_ref[h:h + 1, :] * ga_ref[steps, cols]).astype(o_ref.dtype)
        return carry

    lax.fori_loop(0, q_ref.shape[0] // rows_chunk, chunk_body, 0)

    @pl.when(c_idx == pl.num_programs(1) - 1)
    def _():
        for h in range(heads):
            s_out_ref[h] = st_scr[h].T


def _recur(q, f, v, ga, norm_w, *, n, heads, chunk, chunks_per_step):
    rows_total = q.shape[0]
    l = rows_total // (n * heads)
    nlev = chunk.bit_length() - 1
    rows = chunk * heads
    steps = l // (chunk * chunks_per_step)
    blk = pl.BlockSpec((chunks_per_step * rows, HEAD_DIM), lambda i, c: (i * steps + c, 0))
    row_blk = pl.BlockSpec((chunks_per_step * chunk, heads * HEAD_DIM), lambda i, c: (i * steps + c, 0))
    st_spec = pl.BlockSpec((None, None, heads, HEAD_DIM, HEAD_DIM), lambda i, c: (0, i, 0, 0, 0))
    lvl = jnp.asarray(_split_level_table(chunk))
    return pl.pallas_call(
        functools.partial(_recur_kernel, heads=heads, chunk=chunk),
        grid=(n, steps),
        in_specs=[blk, blk, blk, row_blk, _const_spec((heads, HEAD_DIM)), _const_spec((chunk, chunk))],
        out_specs=[row_blk, st_spec],
        out_shape=[jax.ShapeDtypeStruct(ga.shape, BF16),
                   jax.ShapeDtypeStruct((1, n, heads, HEAD_DIM, HEAD_DIM), F32)],
        scratch_shapes=[pltpu.VMEM((heads, HEAD_DIM, HEAD_DIM), F32),
                        pltpu.VMEM((nlev + 2, rows, HEAD_DIM), F32),
                        pltpu.VMEM((rows, HEAD_DIM), F32)],
        compiler_params=_params("parallel", "arbitrary"),
        name="recur",
    )(q, f, v, ga, norm_w.reshape(heads, HEAD_DIM), lvl)


def _recur_step_kernel(q_ref, f_ref, v_ref, ga_ref, nw_ref, s0_ref, o_ref, s_out_ref,
                       qt_scr, kt_scr, vp_scr, oi_scr, pt_scr, g_scr, *, heads, steps, seqs, pad):
    rows_seq = steps * heads
    zero = jnp.zeros((pad * heads, HEAD_DIM), F32)
    qt_scr[...] = zero
    kt_scr[...] = zero
    vp_scr[...] = zero
    pt_scr[...] = jnp.zeros_like(pt_scr)

    def regroup(t, h):
        return pl.ds(t * heads + h, seqs, stride=rows_seq)

    for t in range(steps):
        for h in range(heads):
            g_scr[regroup(t, h), :] = ga_ref[t, :, h * HEAD_DIM:(h + 1) * HEAD_DIM]

    def seq_body(g, carry):
        base = pl.multiple_of(g * rows_seq, rows_seq)
        blk = pl.ds(base, rows_seq)
        seq = pl.ds(pl.multiple_of(g * heads, heads), heads)
        to3 = lambda a: a.reshape(steps, heads, HEAD_DIM)
        q3, f3, v3 = (jnp.stack([ref[t, seq, :] for t in range(steps)]) for ref in (q_ref, f_ref, v_ref))
        k3 = 1.0 - f3
        p = [f3[0]]
        for t in range(1, steps):
            p.append(p[-1] * f3[t])
        r = [None] * steps
        r[steps - 1] = jnp.ones_like(f3[0])
        for s in range(steps - 2, -1, -1):
            r[s] = r[s + 1] * f3[s + 1]
        intra = []
        for t in range(steps):
            acc = jnp.sum(q3[t] * k3[t], axis=-1, keepdims=True) * v3[t]
            dec = None
            for s in range(t - 1, -1, -1):
                dec = f3[s + 1] if dec is None else dec * f3[s + 1]
                acc = acc + jnp.sum(q3[t] * k3[s] * dec, axis=-1, keepdims=True) * v3[s]
            intra.append(acc)
        for t in range(steps):
            rows_t = pl.ds(t * heads, heads)
            qt_scr[rows_t, :] = q3[t] * p[t]
            kt_scr[rows_t, :] = k3[t] * r[t]
            vp_scr[rows_t, :] = v3[t]
        pt_scr[0:heads, :] = p[steps - 1]
        p_col = pt_scr[...].T
        for h in range(heads):
            rows = _group_rows(h, pad, heads)
            qt = qt_scr[rows, :].astype(BF16)
            kt = kt_scr[rows, :].astype(BF16)
            vb = vp_scr[rows, :].astype(BF16)
            s0 = s0_ref[g, h]
            oi_scr[rows, :] = jnp.dot(qt, s0.astype(BF16), preferred_element_type=F32)
            s_out_ref[g, h] = s0 * p_col[:, h:h + 1] + lax.dot_general(
                kt, vb, TN, preferred_element_type=F32)
        o3 = to3(oi_scr[0:rows_seq, :]) + jnp.stack(intra, axis=0)
        o3 = o3 * lax.rsqrt(jnp.mean(o3 * o3, axis=-1, keepdims=True) + EPS)
        g_scr[blk, :] = (o3 * nw_ref[...] * to3(g_scr[blk, :])).reshape(rows_seq, HEAD_DIM)
        return carry

    lax.fori_loop(0, seqs, seq_body, 0)
    for t in range(steps):
        for h in range(heads):
            o_ref[t, :, h * HEAD_DIM:(h + 1) * HEAD_DIM] = g_scr[regroup(t, h), :].astype(o_ref.dtype)


def _recur_step(q, f, v, ga, norm_w, states, *, layer, heads, steps, seqs, pad):
    n = states.shape[1]
    rows = seqs * steps * heads
    q, f, v = (a.reshape(steps, n * heads, HEAD_DIM) for a in (q, f, v))
    ga = ga.reshape(steps, n, heads * HEAD_DIM)
    blk = pl.BlockSpec((steps, seqs * heads, HEAD_DIM), lambda i: (0, i, 0))
    row_blk = pl.BlockSpec((steps, seqs, heads * HEAD_DIM), lambda i: (0, i, 0))
    st_blk = (None, seqs, heads, HEAD_DIM, HEAD_DIM)
    scr = pltpu.VMEM((pad * heads, HEAD_DIM), F32)
    o, s_new = pl.pallas_call(
        functools.partial(_recur_step_kernel, heads=heads, steps=steps, seqs=seqs, pad=pad),
        grid=(n // seqs,),
        in_specs=[blk, blk, blk, row_blk, _const_spec((heads, HEAD_DIM)),
                  pl.BlockSpec(st_blk, lambda i: (layer, i, 0, 0, 0))],
        out_specs=[row_blk, pl.BlockSpec(st_blk, lambda i: (0, i, 0, 0, 0))],
        out_shape=[jax.ShapeDtypeStruct(ga.shape, BF16), jax.ShapeDtypeStruct((1,) + states.shape[1:], F32)],
        scratch_shapes=[scr, scr, scr, scr, pltpu.VMEM((HEAD_DIM, HEAD_DIM), F32),
                        pltpu.VMEM((rows, HEAD_DIM), F32)],
        compiler_params=_params("parallel"),
        name="recur_step",
    )(q, f, v, ga, norm_w.reshape(heads, HEAD_DIM), states)
    return o.reshape(steps * n, heads * HEAD_DIM), s_new


def _conv_taps(ext_scr, w_ref, b_ref, y_ref, *, ext_row0, out_row0, tb, groups):
    acc = [jnp.broadcast_to(b_ref[...], (tb, groups, LANES)), jnp.zeros((tb, groups, LANES), F32)]
    for j in range(CONV_WIDTH):
        start = (ext_row0 + j) * groups
        if not isinstance(start, int):
            start = pl.multiple_of(start, groups)
        e = ext_scr[pl.ds(start, tb * groups), :].reshape(tb, groups, LANES)
        acc[j % 2] = acc[j % 2] + w_ref[j] * e
    y_ref[pl.ds(out_row0 * groups, tb * groups), :] = (acc[0] + acc[1]).reshape(tb * groups, LANES)


def _conv_kernel(u_ref, w_ref, b_ref, y_ref, buf_out_ref, ext_scr, *, tile, tb, groups):
    t_idx = pl.program_id(1)
    hist = CONV_HIST * groups

    @pl.when(t_idx == 0)
    def _():
        ext_scr[0:hist, :] = jnp.zeros((hist, LANES), F32)

    @pl.when(t_idx > 0)
    def _():
        ext_scr[0:hist, :] = ext_scr[tile * groups:tile * groups + hist, :]

    ext_scr[hist:hist + tile * groups, :] = u_ref[...]

    def block_body(i, carry):
        r0 = pl.multiple_of(i * tb, tb)
        _conv_taps(ext_scr, w_ref, b_ref, y_ref, ext_row0=r0, out_row0=r0, tb=tb, groups=groups)
        return carry

    lax.fori_loop(0, tile // tb, block_body, 0)

    @pl.when(t_idx == pl.num_programs(1) - 1)
    def _():
        for g in range(groups):
            buf_out_ref[:, g * LANES:(g + 1) * LANES] = ext_scr[
                pl.ds(tile * groups + g, CONV_HIST, stride=groups), :]


def _conv(u, dw_w, dw_b, *, n, tile, tb):
    ch = dw_w.shape[1]
    groups = ch // LANES
    l = u.shape[0] // (n * groups)
    steps = l // tile
    blk = pl.BlockSpec((tile * groups, LANES), lambda i, t: (i * steps + t, 0))
    return pl.pallas_call(
        functools.partial(_conv_kernel, tile=tile, tb=tb, groups=groups),
        grid=(n, steps),
        in_specs=[blk, _const_spec((CONV_WIDTH, groups, LANES)), _const_spec((1, groups, LANES))],
        out_specs=[blk, pl.BlockSpec((None, None, CONV_HIST, ch), lambda i, t: (0, i, 0, 0))],
        out_shape=[jax.ShapeDtypeStruct(u.shape, F32), jax.ShapeDtypeStruct((1, n, CONV_HIST, ch), F32)],
        scratch_shapes=[pltpu.VMEM(((tile + CONV_HIST) * groups, LANES), F32)],
        compiler_params=_params("parallel", "arbitrary"),
        name="conv",
    )(u, dw_w.reshape(CONV_WIDTH, groups, LANES), dw_b.reshape(1, groups, LANES))


def _conv_step_kernel(u_ref, w_ref, b_ref, buf_ref, y_ref, buf_out_ref, *, steps, seqs, groups):
    for g in range(groups):
        cols = slice(g * LANES, (g + 1) * LANES)
        rows = _group_rows(g, seqs, groups)
        ext = [buf_ref[t, :, cols] for t in range(CONV_HIST)] + [u_ref[t, rows, :] for t in range(steps)]
        for t in range(steps):
            acc = [jnp.broadcast_to(b_ref[0, g:g + 1, :], (seqs, LANES)), jnp.zeros((seqs, LANES), F32)]
            for j in range(CONV_WIDTH):
                acc[j % 2] = acc[j % 2] + w_ref[j, g:g + 1, :] * ext[t + j]
            y_ref[t, rows, :] = acc[0] + acc[1]
        for t in range(CONV_HIST):
            buf_out_ref[t, :, cols] = ext[t + steps]


def _conv_step(u, dw_w, dw_b, bufs, *, layer, steps, seqs):
    _, _, n, ch = bufs.shape
    groups = ch // LANES
    blk = pl.BlockSpec((steps, seqs * groups, LANES), lambda i: (0, i, 0))
    buf_blk = (None, CONV_HIST, seqs, ch)
    y, buf_new = pl.pallas_call(
        functools.partial(_conv_step_kernel, steps=steps, seqs=seqs, groups=groups),
        grid=(n // seqs,),
        in_specs=[blk, _const_spec((CONV_WIDTH, groups, LANES)), _const_spec((1, groups, LANES)),
                  pl.BlockSpec(buf_blk, lambda i: (layer, 0, i, 0))],
        out_specs=[blk, pl.BlockSpec(buf_blk, lambda i: (0, 0, i, 0))],
        out_shape=[jax.ShapeDtypeStruct((steps, n * groups, LANES), F32),
                   jax.ShapeDtypeStruct((1,) + bufs.shape[1:], F32)],
        compiler_params=_params("parallel"),
        name="conv_step",
    )(u.reshape(steps, n * groups, LANES), dw_w.reshape(CONV_WIDTH, groups, LANES),
      dw_b.reshape(1, groups, LANES), bufs)
    return y.reshape(u.shape), buf_new


def _post_kernel(x_ref, oa_ref, yc_ref, sgb_ref, g1_ref, sh2_ref, sc2_ref, g2_ref, lnw_ref, lnb_ref,
                 wo_ref, nfw_ref, w1_ref, w2_ref, nlw_ref, y_ref, c_scr, h_scr, acc_scr, *, fc):
    tm, d = x_ref.shape
    groups = d // LANES
    for g in range(groups):
        c_scr[:, g * LANES:(g + 1) * LANES] = yc_ref[_group_rows(g, tm, groups), :]
    yc = c_scr[...]
    mu = jnp.mean(yc, axis=-1, keepdims=True)
    dv = yc - mu
    var = jnp.mean(dv * dv, axis=-1, keepdims=True)
    ob = _silu(dv * lax.rsqrt(var + EPS) * lnw_ref[...] + lnb_ref[...]) * sgb_ref[...].astype(F32)
    merged = jnp.dot(oa_ref[...], wo_ref[0:d, :], preferred_element_type=F32)
    merged = merged + jnp.dot(ob.astype(BF16), wo_ref[d:2 * d, :], preferred_element_type=F32)
    x1 = x_ref[...] + _mod_rows(g1_ref, tm) * merged
    ms = jnp.mean(x1 * x1, axis=-1, keepdims=True)
    h2 = x1 * lax.rsqrt(ms + EPS) * nfw_ref[...]
    h2 = h2 * (1.0 + _mod_rows(sc2_ref, tm)) + _mod_rows(sh2_ref, tm)
    h_scr[...] = h2.astype(BF16)
    d_ff = w2_ref.shape[0]
    for c in range(d_ff // fc):
        gate = jnp.dot(h_scr[...], w1_ref[:, c * fc:(c + 1) * fc], preferred_element_type=F32)
        up = jnp.dot(h_scr[...], w1_ref[:, d_ff + c * fc:d_ff + (c + 1) * fc], preferred_element_type=F32)
        part = jnp.dot((_silu(gate) * up).astype(BF16), w2_ref[c * fc:(c + 1) * fc, :],
                       preferred_element_type=F32)
        if c == 0:
            acc_scr[...] = part
        else:
            acc_scr[...] += part
    x2 = x1 + _mod_rows(g2_ref, tm) * acc_scr[...]
    ms2 = jnp.mean(x2 * x2, axis=-1, keepdims=True)
    y_ref[...] = x2 * lax.rsqrt(ms2 + EPS) * nlw_ref[...]


def _post(x, oa, yc, sgb, mod, lnw, lnb, wo, nfw, w1, w2, nlw, *, tm, fc, rows_per_mod):
    m, d = x.shape
    groups = d // LANES
    mod_spec = lambda row: _mod_spec(mod, row, d, tm, rows_per_mod)
    row_spec = pl.BlockSpec((tm, d), lambda i: (i, 0))
    il_spec = pl.BlockSpec((tm * groups, LANES), lambda i: (i, 0))
    vec = _const_spec((1, d))
    return pl.pallas_call(
        functools.partial(_post_kernel, fc=fc),
        grid=(m // tm,),
        in_specs=[row_spec, row_spec, il_spec, row_spec, mod_spec(2), mod_spec(3), mod_spec(4), mod_spec(5),
                  vec, vec, _const_spec(wo.shape), vec, _const_spec(w1.shape), _const_spec(w2.shape), vec],
        out_specs=row_spec,
        out_shape=jax.ShapeDtypeStruct((m, d), F32),
        scratch_shapes=[pltpu.VMEM((tm, d), F32), pltpu.VMEM((tm, d), BF16), pltpu.VMEM((tm, d), F32)],
        compiler_params=_params("parallel"),
        name="post",
    )(x, oa, yc, sgb, mod, mod, mod, mod, lnw, lnb, wo, nfw, w1, w2, nlw)


INPROJ_TC = 256
FFN_CHUNK = 256
MOD_TN = 1024
PROMPT_TM = 512
PROMPT_CHUNK = 128
PROMPT_CHUNKS_PER_STEP = 4
PROMPT_CONV_TILE = 512
PROMPT_CONV_TB = 16
SAMPLE_SEQS = 16
SAMPLE_PAD = 16


def kernel(x_prompt, x_sample, state_hgrn, state_conv, c_prompt, c_sample, w_ada, b_ada, norm_mix_w, w_in, b_in, lb_logits, hgrn_norm_w, conv_dw_w, conv_dw_b, conv_ln_w, conv_ln_b, w_out, norm_ffn_w, w_ffn_in, w_ffn_out, norm_final_w):
    depth = w_in.shape[0]
    assert depth == 1, "single-layer configuration"
    layer = 0
    n_p, l_p, d = x_prompt.shape
    n_s, l_s, _ = x_sample.shape
    heads = d // HEAD_DIM

    w_in_b = w_in[layer].astype(BF16)
    w1 = w_ffn_in[layer].astype(BF16)
    w2 = w_ffn_out[layer].astype(BF16)
    wo = w_out[layer].astype(BF16)
    row = lambda a: a.reshape(1, -1)

    mod = _mod(jnp.concatenate([c_prompt, c_sample], axis=0), w_ada[layer], b_ada[layer], tn=MOD_TN)
    mod_p = mod[:n_p].reshape(n_p, 1, MOD_ROWS * d)
    mod_s = mod[n_p:].reshape(1, n_s, MOD_ROWS * d)

    def inproj(x2, mod_g, tm):
        return _inproj(x2, mod_g, row(norm_mix_w[layer]), w_in_b, row(b_in[layer]), lb_logits,
                       layer=layer, tm=tm, tc=INPROJ_TC, rows_per_mod=tm if mod_g.shape[1] > 1 else l_p)

    def post(x2, oa, yc, sgb, mod_g, tm):
        return _post(x2, oa, yc, sgb, mod_g, row(conv_ln_w[layer]), row(conv_ln_b[layer]), wo,
                     row(norm_ffn_w[layer]), w1, w2, row(norm_final_w),
                     tm=tm, fc=FFN_CHUNK, rows_per_mod=tm if mod_g.shape[1] > 1 else l_p)

    xp = x_prompt.reshape(n_p * l_p, d)
    q, f, v, ga, u, sgb = inproj(xp, mod_p, PROMPT_TM)
    oa, s_p = _recur(q, f, v, ga, hgrn_norm_w[layer], n=n_p, heads=heads, chunk=PROMPT_CHUNK,
                     chunks_per_step=PROMPT_CHUNKS_PER_STEP)
    yc, b_p = _conv(u, conv_dw_w[layer], conv_dw_b[layer], n=n_p, tile=PROMPT_CONV_TILE, tb=PROMPT_CONV_TB)
    y_p = post(xp, oa, yc, sgb, mod_p, PROMPT_TM).reshape(n_p, l_p, d)

    xs = x_sample.transpose(1, 0, 2).reshape(l_s * n_s, d)
    q, f, v, ga, u, sgb = inproj(xs, mod_s, n_s * l_s)
    oa, s_s = _recur_step(q, f, v, ga, hgrn_norm_w[layer], state_hgrn, layer=layer,
                          heads=heads, steps=l_s, seqs=SAMPLE_SEQS, pad=SAMPLE_PAD)
    yc, b_s = _conv_step(u, conv_dw_w[layer], conv_dw_b[layer], state_conv.transpose(0, 2, 1, 3),
                         layer=layer, steps=l_s, seqs=SAMPLE_SEQS)
    y_s = post(xs, oa, yc, sgb, mod_s, n_s * l_s).reshape(l_s, n_s, d).transpose(1, 0, 2)

    return (y_p, y_s, s_p, b_p, s_s, b_s.transpose(0, 2, 1, 3))
```

```python
import functools

import numpy as np
import jax
import jax.numpy as jnp
from jax import lax
from jax.experimental import pallas as pl
from jax.experimental.pallas import tpu as pltpu

F32 = jnp.float32
BF16 = jnp.bfloat16

EPS = 1e-6
LANES = 128
SUBLANES = 8
HEAD_DIM = LANES
CONV_WIDTH = 31
CONV_HIST = CONV_WIDTH - 1
VMEM_LIMIT_BYTES = 56 * 1024 * 1024

IN_SEGMENTS = 8
MOD_ROWS = 6

NT = (((1,), (1,)), ((), ()))
TN = (((0,), (0,)), ((), ()))


def _silu(x):
    return x * jax.nn.sigmoid(x)


def _params(*sem):
    return pltpu.CompilerParams(dimension_semantics=sem, vmem_limit_bytes=VMEM_LIMIT_BYTES)


def _const_spec(shape):
    nd = len(shape)
    return pl.BlockSpec(shape, lambda *_: (0,) * nd, pipeline_mode=pl.Buffered(1))


def _mod_rows(ref, tm):
    m = ref[...]
    r = m.shape[0]
    return m if r in (1, tm) else jnp.tile(m, (tm // r, 1))


def _group_rows(g, n, groups):
    return pl.ds(g, n, stride=groups)


def _mod_kernel(c_ref, w_ref, b_ref, o_ref):
    a = _silu(c_ref[...]).astype(BF16)
    o_ref[...] = jnp.dot(a, w_ref[...].astype(BF16), preferred_element_type=F32) + b_ref[...]


def _mod(c, w_ada, b_ada, *, tn=512):
    n, d = c.shape
    cols = w_ada.shape[1]
    return pl.pallas_call(
        _mod_kernel,
        grid=(cols // tn,),
        in_specs=[pl.BlockSpec((n, d), lambda j: (0, 0)),
                  pl.BlockSpec((d, tn), lambda j: (0, j)),
                  pl.BlockSpec((1, tn), lambda j: (0, j))],
        out_specs=pl.BlockSpec((n, tn), lambda j: (0, j)),
        out_shape=jax.ShapeDtypeStruct((n, cols), F32),
        compiler_params=_params("parallel"),
        name="mod",
    )(c, w_ada, b_ada.reshape(1, cols))


def _inproj_kernel(x_ref, sh_ref, sc_ref, nw_ref, w_ref, b_ref, lbl_ref,
                   q_ref, f_ref, v_ref, ga_ref, u_ref, sgb_ref, h_scr, *, layer, tc):
    tm, d = x_ref.shape
    groups = d // LANES
    x = x_ref[...]
    ms = jnp.mean(x * x, axis=-1, keepdims=True)
    h = x * lax.rsqrt(ms + EPS) * nw_ref[...]
    h = h * (1.0 + _mod_rows(sc_ref, tm)) + _mod_rows(sh_ref, tm)
    h_scr[...] = h.astype(BF16)
    for j in range(d // tc):
        def seg(s):
            cols = slice(s * d + j * tc, s * d + (j + 1) * tc)
            return jnp.dot(h_scr[...], w_ref[:, cols], preferred_element_type=F32) + b_ref[:, cols]

        def put(ref, val):
            for g in range(tc // LANES):
                rows = _group_rows(j * (tc // LANES) + g, tm, groups)
                ref[rows, :] = val[:, g * LANES:(g + 1) * LANES]

        put(q_ref, _silu(seg(0)))
        lg = lbl_ref[:, j * tc:(j + 1) * tc]
        e = jnp.exp(lg - jnp.max(lg, axis=0, keepdims=True))
        lb = jnp.sum(e[:layer + 1], axis=0, keepdims=True) / jnp.sum(e, axis=0, keepdims=True)
        put(f_ref, lb + (1.0 - lb) * jax.nn.sigmoid(seg(1)))
        put(v_ref, seg(2))
        og_act = _silu(seg(3))
        ga_ref[:, j * tc:(j + 1) * tc] = og_act * jax.nn.sigmoid(seg(6))
        glu_a = seg(4)
        put(u_ref, glu_a * jax.nn.sigmoid(seg(5)))
        sgb_ref[:, j * tc:(j + 1) * tc] = jax.nn.sigmoid(seg(7)).astype(sgb_ref.dtype)


def _mod_spec(mod, row, d, tm, rows_per_mod):
    tiles_per_mod = rows_per_mod // tm
    return pl.BlockSpec((None, mod.shape[1], d), lambda i: (i // tiles_per_mod, 0, row))


def _inproj(x, mod, nw, w, b, lbl, *, layer, tm, tc, rows_per_mod):
    m, d = x.shape
    groups = d // LANES
    mod_spec = lambda row: _mod_spec(mod, row, d, tm, rows_per_mod)
    row_spec = pl.BlockSpec((tm, d), lambda i: (i, 0))
    il_spec = pl.BlockSpec((tm * groups, LANES), lambda i: (i, 0))
    il_shape = jax.ShapeDtypeStruct((m * groups, LANES), F32)
    return pl.pallas_call(
        functools.partial(_inproj_kernel, layer=layer, tc=tc),
        grid=(m // tm,),
        in_specs=[row_spec, mod_spec(0), mod_spec(1),
                  _const_spec((1, d)), _const_spec(w.shape), _const_spec(b.shape),
                  _const_spec(lbl.shape)],
        out_specs=[il_spec, il_spec, il_spec, row_spec, il_spec, row_spec],
        out_shape=[il_shape, il_shape, il_shape, jax.ShapeDtypeStruct((m, d), F32), il_shape,
                   jax.ShapeDtypeStruct((m, d), BF16)],
        scratch_shapes=[pltpu.VMEM((tm, d), BF16)],
        compiler_params=_params("parallel"),
        name="inproj",
    )(x, mod, mod, nw, w, b, lbl)


def _split_level_table(c):
    t = np.arange(c)[:, None]
    s = np.arange(c)[None, :]
    x = t ^ s
    lvl = np.zeros((c, c), np.int32)
    for l in range(1, c.bit_length()):
        lvl[(x >> (l - 1)) == 1] = l
    return np.where(t > s, lvl, 0).astype(np.int32)


def _recur_kernel(q_ref, f_ref, v_ref, ga_ref, nw_ref, lvl_ref, o_ref, s_out_ref,
                  st_scr, x_scr, d_scr, pt_scr, *, heads, chunk):
    c_idx = pl.program_id(1)
    nlev = chunk.bit_length() - 1

    @pl.when(c_idx == 0)
    def _():
        st_scr[...] = jnp.zeros_like(st_scr)
        pt_scr[...] = jnp.zeros_like(pt_scr)

    def to3(a):
        return a.reshape(chunk, heads, HEAD_DIM)

    def flat(a):
        return a.reshape(chunk * heads, HEAD_DIM)

    lvl = lvl_ref[...]
    rows_chunk = chunk * heads

    def chunk_body(ci, carry):
        base = pl.multiple_of(ci * rows_chunk, rows_chunk)
        whole = pl.ds(base, rows_chunk)
        steps = pl.ds(pl.multiple_of(ci * chunk, chunk), chunk)
        f3 = to3(f_ref[whole, :])
        q3 = to3(q_ref[whole, :])
        v3 = to3(v_ref[whole, :])
        k3 = 1.0 - f3
        d_scr[...] = flat(jnp.sum(q3 * k3, axis=-1, keepdims=True) * v3)
        qp, kr, tot = q3 * f3, k3, f3
        half = 1
        for l in range(nlev):
            blk = 2 * half
            shp = (chunk // blk, blk, heads, HEAD_DIM)
            qp4, kr4 = qp.reshape(shp), kr.reshape(shp)
            x_scr[l] = flat(jnp.concatenate([kr4[:, :half], qp4[:, half:]], axis=1))
            tot2 = tot.reshape(chunk // blk, 2, heads, HEAD_DIM)
            left_tot, right_tot = tot2[:, 0:1], tot2[:, 1:2]
            qp = jnp.concatenate([qp4[:, :half], qp4[:, half:] * left_tot], axis=1).reshape(f3.shape)
            kr = jnp.concatenate([kr4[:, :half] * right_tot, kr4[:, half:]], axis=1).reshape(f3.shape)
            tot = (left_tot * right_tot).reshape(chunk // blk, heads, HEAD_DIM)
            half = blk
        x_scr[nlev] = flat(qp)
        x_scr[nlev + 1] = flat(kr)
        pt_scr[0:heads, :] = tot[0]
        p_col = pt_scr[...].T

        for h in range(heads):
            rows = _group_rows(h, chunk, heads)
            blk_rows = _group_rows(base + h, chunk, heads)
            att = jnp.zeros((chunk, chunk), F32)
            for l in range(nlev):
                xb = x_scr[l, rows, :].astype(BF16)
                a = lax.dot_general(xb, xb, NT, preferred_element_type=F32)
                att = jnp.where(lvl == l + 1, a, att)
            qt = x_scr[nlev, rows, :].astype(BF16)
            kt = x_scr[nlev + 1, rows, :].astype(BF16)
            vb = v_ref[blk_rows, :].astype(BF16)
            st0 = st_scr[h]
            lhs = jnp.concatenate([att.astype(BF16), qt], axis=1)
            rhs = jnp.concatenate([vb, st0.astype(BF16)], axis=0)
            o = jnp.dot(lhs, rhs, preferred_element_type=F32) + d_scr[rows, :]
            st_scr[h] = st0 * p_col[:, h:h + 1] + lax.dot_general(kt, vb, TN,
                                                                  preferred_element_type=F32)
            o = o * lax.rsqrt(jnp.mean(o * o, axis=-1, keepdims=True) + EPS)
            cols = slice(h * HEAD_DIM, (h + 1) * HEAD_DIM)
            o_ref[steps, cols] = (o * nw_ref[h:h + 1, :] * ga_ref[steps, cols]).astype(o_ref.dtype)
        return carry

    lax.fori_loop(0, q_ref.shape[0] // rows_chunk, chunk_body, 0)

    @pl.when(c_idx == pl.num_programs(1) - 1)
    def _():
        s_out_ref[...] = st_scr[...]


def _recur(q, f, v, ga, norm_w, *, n, heads, chunk, chunks_per_step):
    rows_total = q.shape[0]
    l = rows_total // (n * heads)
    nlev = chunk.bit_length() - 1
    rows = chunk * heads
    steps = l // (chunk * chunks_per_step)
    blk = pl.BlockSpec((chunks_per_step * rows, HEAD_DIM), lambda i, c: (i * steps + c, 0))
    row_blk = pl.BlockSpec((chunks_per_step * chunk, heads * HEAD_DIM), lambda i, c: (i * steps + c, 0))
    st_spec = pl.BlockSpec((None, None, heads, HEAD_DIM, HEAD_DIM), lambda i, c: (0, i, 0, 0, 0))
    lvl = jnp.asarray(_split_level_table(chunk))
    return pl.pallas_call(
        functools.partial(_recur_kernel, heads=heads, chunk=chunk),
        grid=(n, steps),
        in_specs=[blk, blk, blk, row_blk, _const_spec((heads, HEAD_DIM)), _const_spec((chunk, chunk))],
        out_specs=[row_blk, st_spec],
        out_shape=[jax.ShapeDtypeStruct(ga.shape, BF16),
                   jax.ShapeDtypeStruct((1, n, heads, HEAD_DIM, HEAD_DIM), F32)],
        scratch_shapes=[pltpu.VMEM((heads, HEAD_DIM, HEAD_DIM), F32),
                        pltpu.VMEM((nlev + 2, rows, HEAD_DIM), F32),
                        pltpu.VMEM((rows, HEAD_DIM), F32),
                        pltpu.VMEM((HEAD_DIM, HEAD_DIM), F32)],
        compiler_params=_params("parallel", "arbitrary"),
        name="recur",
    )(q, f, v, ga, norm_w.reshape(heads, HEAD_DIM), lvl)


def _recur_step_kernel(q_ref, f_ref, v_ref, ga_ref, nw_ref, s0_ref, o_ref, s_out_ref,
                       qt_scr, kt_scr, vp_scr, oi_scr, pt_scr, g_scr, *, heads, steps, seqs, pad):
    rows_seq = steps * heads
    zero = jnp.zeros((pad * heads, HEAD_DIM), F32)
    qt_scr[...] = zero
    kt_scr[...] = zero
    vp_scr[...] = zero
    pt_scr[...] = jnp.zeros_like(pt_scr)

    def regroup(t, h):
        return pl.ds(t * heads + h, seqs, stride=rows_seq)

    for t in range(steps):
        for h in range(heads):
            g_scr[regroup(t, h), :] = ga_ref[t, :, h * HEAD_DIM:(h + 1) * HEAD_DIM]

    def seq_body(g, carry):
        base = pl.multiple_of(g * rows_seq, rows_seq)
        blk = pl.ds(base, rows_seq)
        seq = pl.ds(pl.multiple_of(g * heads, heads), heads)
        to3 = lambda a: a.reshape(steps, heads, HEAD_DIM)
        q3, f3, v3 = (jnp.stack([ref[t, seq, :] for t in range(steps)]) for ref in (q_ref, f_ref, v_ref))
        k3 = 1.0 - f3
        p = [f3[0]]
        for t in range(1, steps):
            p.append(p[-1] * f3[t])
        r = [None] * steps
        r[steps - 1] = jnp.ones_like(f3[0])
        for s in range(steps - 2, -1, -1):
            r[s] = r[s + 1] * f3[s + 1]
        intra = []
        for t in range(steps):
            acc = jnp.sum(q3[t] * k3[t], axis=-1, keepdims=True) * v3[t]
            dec = None
            for s in range(t - 1, -1, -1):
                dec = f3[s + 1] if dec is None else dec * f3[s + 1]
                acc = acc + jnp.sum(q3[t] * k3[s] * dec, axis=-1, keepdims=True) * v3[s]
            intra.append(acc)
        for t in range(steps):
            rows_t = pl.ds(t * heads, heads)
            qt_scr[rows_t, :] = q3[t] * p[t]
            kt_scr[rows_t, :] = k3[t] * r[t]
            vp_scr[rows_t, :] = v3[t]
        pt_scr[0:heads, :] = p[steps - 1]
        p_col = pt_scr[...].T
        for h in range(heads):
            rows = _group_rows(h, pad, heads)
            qt = qt_scr[rows, :].astype(BF16)
            kt = kt_scr[rows, :].astype(BF16)
            vb = vp_scr[rows, :].astype(BF16)
            s0 = s0_ref[g, h]
            oi_scr[rows, :] = jnp.dot(qt, s0.astype(BF16), preferred_element_type=F32)
            s_out_ref[g, h] = s0 * p_col[:, h:h + 1] + lax.dot_general(
                kt, vb, TN, preferred_element_type=F32)
        o3 = to3(oi_scr[0:rows_seq, :]) + jnp.stack(intra, axis=0)
        o3 = o3 * lax.rsqrt(jnp.mean(o3 * o3, axis=-1, keepdims=True) + EPS)
        g_scr[blk, :] = (o3 * nw_ref[...] * to3(g_scr[blk, :])).reshape(rows_seq, HEAD_DIM)
        return carry

    lax.fori_loop(0, seqs, seq_body, 0)
    for t in range(steps):
        for h in range(heads):
            o_ref[t, :, h * HEAD_DIM:(h + 1) * HEAD_DIM] = g_scr[regroup(t, h), :].astype(o_ref.dtype)


def _recur_step(q, f, v, ga, norm_w, states, *, layer, heads, steps, seqs, pad):
    n = states.shape[1]
    rows = seqs * steps * heads
    q, f, v = (a.reshape(steps, n * heads, HEAD_DIM) for a in (q, f, v))
    ga = ga.reshape(steps, n, heads * HEAD_DIM)
    blk = pl.BlockSpec((steps, seqs * heads, HEAD_DIM), lambda i: (0, i, 0))
    row_blk = pl.BlockSpec((steps, seqs, heads * HEAD_DIM), lambda i: (0, i, 0))
    st_blk = (None, seqs, heads, HEAD_DIM, HEAD_DIM)
    scr = pltpu.VMEM((pad * heads, HEAD_DIM), F32)
    o, s_new = pl.pallas_call(
        functools.partial(_recur_step_kernel, heads=heads, steps=steps, seqs=seqs, pad=pad),
        grid=(n // seqs,),
        in_specs=[blk, blk, blk, row_blk, _const_spec((heads, HEAD_DIM)),
                  pl.BlockSpec(st_blk, lambda i: (layer, i, 0, 0, 0))],
        out_specs=[row_blk, pl.BlockSpec(st_blk, lambda i: (0, i, 0, 0, 0))],
        out_shape=[jax.ShapeDtypeStruct(ga.shape, BF16), jax.ShapeDtypeStruct((1,) + states.shape[1:], F32)],
        scratch_shapes=[scr, scr, scr, scr, pltpu.VMEM((HEAD_DIM, HEAD_DIM), F32),
                        pltpu.VMEM((rows, HEAD_DIM), F32)],
        compiler_params=_params("parallel"),
        name="recur_step",
    )(q, f, v, ga, norm_w.reshape(heads, HEAD_DIM), states)
    return o.reshape(steps * n, heads * HEAD_DIM), s_new


def _conv_taps(ext_scr, w_ref, b_ref, y_ref, *, ext_row0, out_row0, tb, groups):
    acc = [jnp.broadcast_to(b_ref[...], (tb, groups, LANES)), jnp.zeros((tb, groups, LANES), F32)]
    for j in range(CONV_WIDTH):
        start = (ext_row0 + j) * groups
        if not isinstance(start, int):
            start = pl.multiple_of(start, groups)
        e = ext_scr[pl.ds(start, tb * groups), :].reshape(tb, groups, LANES)
        acc[j % 2] = acc[j % 2] + w_ref[j] * e
    y_ref[pl.ds(out_row0 * groups, tb * groups), :] = (acc[0] + acc[1]).reshape(tb * groups, LANES)


def _conv_kernel(u_ref, w_ref, b_ref, y_ref, buf_out_ref, ext_scr, *, tile, tb, groups):
    t_idx = pl.program_id(1)
    hist = CONV_HIST * groups

    @pl.when(t_idx == 0)
    def _():
        ext_scr[0:hist, :] = jnp.zeros((hist, LANES), F32)

    @pl.when(t_idx > 0)
    def _():
        ext_scr[0:hist, :] = ext_scr[tile * groups:tile * groups + hist, :]

    ext_scr[hist:hist + tile * groups, :] = u_ref[...]

    def block_body(i, carry):
        r0 = pl.multiple_of(i * tb, tb)
        _conv_taps(ext_scr, w_ref, b_ref, y_ref, ext_row0=r0, out_row0=r0, tb=tb, groups=groups)
        return carry

    lax.fori_loop(0, tile // tb, block_body, 0)

    @pl.when(t_idx == pl.num_programs(1) - 1)
    def _():
        for g in range(groups):
            buf_out_ref[:, g * LANES:(g + 1) * LANES] = ext_scr[
                pl.ds(tile * groups + g, CONV_HIST, stride=groups), :]


def _conv(u, dw_w, dw_b, *, n, tile, tb):
    ch = dw_w.shape[1]
    groups = ch // LANES
    l = u.shape[0] // (n * groups)
    steps = l // tile
    blk = pl.BlockSpec((tile * groups, LANES), lambda i, t: (i * steps + t, 0))
    return pl.pallas_call(
        functools.partial(_conv_kernel, tile=tile, tb=tb, groups=groups),
        grid=(n, steps),
        in_specs=[blk, _const_spec((CONV_WIDTH, groups, LANES)), _const_spec((1, groups, LANES))],
        out_specs=[blk, pl.BlockSpec((None, None, CONV_HIST, ch), lambda i, t: (0, i, 0, 0))],
        out_shape=[jax.ShapeDtypeStruct(u.shape, F32), jax.ShapeDtypeStruct((1, n, CONV_HIST, ch), F32)],
        scratch_shapes=[pltpu.VMEM(((tile + CONV_HIST) * groups, LANES), F32)],
        compiler_params=_params("parallel", "arbitrary"),
        name="conv",
    )(u, dw_w.reshape(CONV_WIDTH, groups, LANES), dw_b.reshape(1, groups, LANES))


def _conv_step_kernel(u_ref, w_ref, b_ref, buf_ref, y_ref, buf_out_ref, *, steps, seqs, groups):
    for g in range(groups):
        cols = slice(g * LANES, (g + 1) * LANES)
        rows = _group_rows(g, seqs, groups)
        ext = [buf_ref[t, :, cols] for t in range(CONV_HIST)] + [u_ref[t, rows, :] for t in range(steps)]
        for t in range(steps):
            acc = [jnp.broadcast_to(b_ref[0, g:g + 1, :], (seqs, LANES)), jnp.zeros((seqs, LANES), F32)]
            for j in range(CONV_WIDTH):
                acc[j % 2] = acc[j % 2] + w_ref[j, g:g + 1, :] * ext[t + j]
            y_ref[t, rows, :] = acc[0] + acc[1]
        for t in range(CONV_HIST):
            buf_out_ref[t, :, cols] = ext[t + steps]


def _conv_step(u, dw_w, dw_b, bufs, *, layer, steps, seqs):
    _, _, n, ch = bufs.shape
    groups = ch // LANES
    blk = pl.BlockSpec((steps, seqs * groups, LANES), lambda i: (0, i, 0))
    buf_blk = (None, CONV_HIST, seqs, ch)
    y, buf_new = pl.pallas_call(
        functools.partial(_conv_step_kernel, steps=steps, seqs=seqs, groups=groups),
        grid=(n // seqs,),
        in_specs=[blk, _const_spec((CONV_WIDTH, groups, LANES)), _const_spec((1, groups, LANES)),
                  pl.BlockSpec(buf_blk, lambda i: (layer, 0, i, 0))],
        out_specs=[blk, pl.BlockSpec(buf_blk, lambda i: (0, 0, i, 0))],
        out_shape=[jax.ShapeDtypeStruct((steps, n * groups, LANES), F32),
                   jax.ShapeDtypeStruct((1,) + bufs.shape[1:], F32)],
        compiler_params=_params("parallel"),
        name="conv_step",
    )(u.reshape(steps, n * groups, LANES), dw_w.reshape(CONV_WIDTH, groups, LANES),
      dw_b.reshape(1, groups, LANES), bufs)
    return y.reshape(u.shape), buf_new


def _post_kernel(x_ref, oa_ref, yc_ref, sgb_ref, g1_ref, sh2_ref, sc2_ref, g2_ref, lnw_ref, lnb_ref,
                 wo_ref, nfw_ref, w1_ref, w2_ref, nlw_ref, y_ref, c_scr, h_scr, acc_scr, *, fc):
    tm, d = x_ref.shape
    groups = d // LANES
    for g in range(groups):
        c_scr[:, g * LANES:(g + 1) * LANES] = yc_ref[_group_rows(g, tm, groups), :]
    yc = c_scr[...]
    mu = jnp.mean(yc, axis=-1, keepdims=True)
    dv = yc - mu
    var = jnp.mean(dv * dv, axis=-1, keepdims=True)
    ob = _silu(dv * lax.rsqrt(var + EPS) * lnw_ref[...] + lnb_ref[...]) * sgb_ref[...].astype(F32)
    merged = jnp.dot(oa_ref[...], wo_ref[0:d, :], preferred_element_type=F32)
    merged = merged + jnp.dot(ob.astype(BF16), wo_ref[d:2 * d, :], preferred_element_type=F32)
    x1 = x_ref[...] + _mod_rows(g1_ref, tm) * merged
    ms = jnp.mean(x1 * x1, axis=-1, keepdims=True)
    h2 = x1 * lax.rsqrt(ms + EPS) * nfw_ref[...]
    h2 = h2 * (1.0 + _mod_rows(sc2_ref, tm)) + _mod_rows(sh2_ref, tm)
    h_scr[...] = h2.astype(BF16)
    d_ff = w2_ref.shape[0]
    for c in range(d_ff // fc):
        gate = jnp.dot(h_scr[...], w1_ref[:, c * fc:(c + 1) * fc], preferred_element_type=F32)
        up = jnp.dot(h_scr[...], w1_ref[:, d_ff + c * fc:d_ff + (c + 1) * fc], preferred_element_type=F32)
        part = jnp.dot((_silu(gate) * up).astype(BF16), w2_ref[c * fc:(c + 1) * fc, :],
                       preferred_element_type=F32)
        if c == 0:
            acc_scr[...] = part
        else:
            acc_scr[...] += part
    x2 = x1 + _mod_rows(g2_ref, tm) * acc_scr[...]
    ms2 = jnp.mean(x2 * x2, axis=-1, keepdims=True)
    y_ref[...] = x2 * lax.rsqrt(ms2 + EPS) * nlw_ref[...]


def _post(x, oa, yc, sgb, mod, lnw, lnb, wo, nfw, w1, w2, nlw, *, tm, fc, rows_per_mod):
    m, d = x.shape
    groups = d // LANES
    mod_spec = lambda row: _mod_spec(mod, row, d, tm, rows_per_mod)
    row_spec = pl.BlockSpec((tm, d), lambda i: (i, 0))
    il_spec = pl.BlockSpec((tm * groups, LANES), lambda i: (i, 0))
    vec = _const_spec((1, d))
    return pl.pallas_call(
        functools.partial(_post_kernel, fc=fc),
        grid=(m // tm,),
        in_specs=[row_spec, row_spec, il_spec, row_spec, mod_spec(2), mod_spec(3), mod_spec(4), mod_spec(5),
                  vec, vec, _const_spec(wo.shape), vec, _const_spec(w1.shape), _const_spec(w2.shape), vec],
        out_specs=row_spec,
        out_shape=jax.ShapeDtypeStruct((m, d), F32),
        scratch_shapes=[pltpu.VMEM((tm, d), F32), pltpu.VMEM((tm, d), BF16), pltpu.VMEM((tm, d), F32)],
        compiler_params=_params("parallel"),
        name="post",
    )(x, oa, yc, sgb, mod, mod, mod, mod, lnw, lnb, wo, nfw, w1, w2, nlw)


INPROJ_TC = 256
FFN_CHUNK = 256
MOD_TN = 1024
PROMPT_TM = 512
PROMPT_CHUNK = 128
PROMPT_CHUNKS_PER_STEP = 4
PROMPT_CONV_TILE = 512
PROMPT_CONV_TB = 16
SAMPLE_SEQS = 16
SAMPLE_PAD = 16


def kernel(x_prompt, x_sample, state_hgrn, state_conv, c_prompt, c_sample, w_ada, b_ada, norm_mix_w, w_in, b_in, lb_logits, hgrn_norm_w, conv_dw_w, conv_dw_b, conv_ln_w, conv_ln_b, w_out, norm_ffn_w, w_ffn_in, w_ffn_out, norm_final_w):
    depth = w_in.shape[0]
    assert depth == 1, "single-layer configuration"
    layer = 0
    n_p, l_p, d = x_prompt.shape
    n_s, l_s, _ = x_sample.shape
    heads = d // HEAD_DIM

    w_in_b = w_in[layer].astype(BF16)
    w1 = w_ffn_in[layer].astype(BF16)
    w2 = w_ffn_out[layer].astype(BF16)
    wo = w_out[layer].astype(BF16)
    row = lambda a: a.reshape(1, -1)

    mod = _mod(jnp.concatenate([c_prompt, c_sample], axis=0), w_ada[layer], b_ada[layer], tn=MOD_TN)
    mod_p = mod[:n_p].reshape(n_p, 1, MOD_ROWS * d)
    mod_s = mod[n_p:].reshape(1, n_s, MOD_ROWS * d)

    def inproj(x2, mod_g, tm):
        return _inproj(x2, mod_g, row(norm_mix_w[layer]), w_in_b, row(b_in[layer]), lb_logits,
                       layer=layer, tm=tm, tc=INPROJ_TC, rows_per_mod=tm if mod_g.shape[1] > 1 else l_p)

    def post(x2, oa, yc, sgb, mod_g, tm):
        return _post(x2, oa, yc, sgb, mod_g, row(conv_ln_w[layer]), row(conv_ln_b[layer]), wo,
                     row(norm_ffn_w[layer]), w1, w2, row(norm_final_w),
                     tm=tm, fc=FFN_CHUNK, rows_per_mod=tm if mod_g.shape[1] > 1 else l_p)

    xp = x_prompt.reshape(n_p * l_p, d)
    q, f, v, ga, u, sgb = inproj(xp, mod_p, PROMPT_TM)
    oa, s_p = _recur(q, f, v, ga, hgrn_norm_w[layer], n=n_p, heads=heads, chunk=PROMPT_CHUNK,
                     chunks_per_step=PROMPT_CHUNKS_PER_STEP)
    yc, b_p = _conv(u, conv_dw_w[layer], conv_dw_b[layer], n=n_p, tile=PROMPT_CONV_TILE, tb=PROMPT_CONV_TB)
    y_p = post(xp, oa, yc, sgb, mod_p, PROMPT_TM).reshape(n_p, l_p, d)

    xs = x_sample.transpose(1, 0, 2).reshape(l_s * n_s, d)
    q, f, v, ga, u, sgb = inproj(xs, mod_s, n_s * l_s)
    oa, s_s = _recur_step(q, f, v, ga, hgrn_norm_w[layer], state_hgrn, layer=layer,
                          heads=heads, steps=l_s, seqs=SAMPLE_SEQS, pad=SAMPLE_PAD)
    yc, b_s = _conv_step(u, conv_dw_w[layer], conv_dw_b[layer], state_conv.transpose(0, 2, 1, 3),
                         layer=layer, steps=l_s, seqs=SAMPLE_SEQS)
    y_s = post(xs, oa, yc, sgb, mod_s, n_s * l_s).reshape(l_s, n_s, d).transpose(1, 0, 2)

    return (y_p, y_s, s_p, b_p, s_s, b_s.transpose(0, 2, 1, 3))
```

```python
import functools

import numpy as np
import jax
import jax.numpy as jnp
from jax import lax
from jax.experimental import pallas as pl
from jax.experimental.pallas import tpu as pltpu

F32 = jnp.float32
BF16 = jnp.bfloat16

EPS = 1e-6
LANES = 128
HEAD_DIM = LANES
CONV_WIDTH = 31
CONV_HIST = CONV_WIDTH - 1
VMEM_LIMIT_BYTES = 56 * 1024 * 1024

MOD_ROWS = 6

NT = (((1,), (1,)), ((), ()))
TN = (((0,), (0,)), ((), ()))


def _silu(x):
    return x * jax.nn.sigmoid(x)


def _params(*sem):
    return pltpu.CompilerParams(dimension_semantics=sem, vmem_limit_bytes=VMEM_LIMIT_BYTES)


def _const_spec(shape):
    nd = len(shape)
    return pl.BlockSpec(shape, lambda *_: (0,) * nd, pipeline_mode=pl.Buffered(1))


def _mod_rows(ref, tm):
    m = ref[...]
    r = m.shape[0]
    return m if r in (1, tm) else jnp.tile(m, (tm // r, 1))


def _group_rows(g, n, groups):
    return pl.ds(g, n, stride=groups)


def _mod_kernel(c_ref, w_ref, b_ref, o_ref):
    a = _silu(c_ref[...]).astype(BF16)
    o_ref[...] = jnp.dot(a, w_ref[...].astype(BF16), preferred_element_type=F32) + b_ref[...]


def _mod(c, w_ada, b_ada, *, tn):
    n, d = c.shape
    cols = w_ada.shape[1]
    return pl.pallas_call(
        _mod_kernel,
        grid=(cols // tn,),
        in_specs=[pl.BlockSpec((n, d), lambda j: (0, 0)),
                  pl.BlockSpec((d, tn), lambda j: (0, j)),
                  pl.BlockSpec((1, tn), lambda j: (0, j))],
        out_specs=pl.BlockSpec((n, tn), lambda j: (0, j)),
        out_shape=jax.ShapeDtypeStruct((n, cols), F32),
        compiler_params=_params("parallel"),
        name="mod",
    )(c, w_ada, b_ada.reshape(1, cols))


def _inproj_kernel(x_ref, sh_ref, sc_ref, nw_ref, w_ref, b_ref, lbl_ref,
                   q_ref, f_ref, v_ref, ga_ref, u_ref, sgb_ref, h_scr, *, layer, tc):
    tm, d = x_ref.shape
    groups = d // LANES
    x = x_ref[...]
    ms = jnp.mean(x * x, axis=-1, keepdims=True)
    h = x * lax.rsqrt(ms + EPS) * nw_ref[...]
    h = h * (1.0 + _mod_rows(sc_ref, tm)) + _mod_rows(sh_ref, tm)
    h_scr[...] = h.astype(BF16)
    for j in range(d // tc):
        def seg(s):
            cols = slice(s * d + j * tc, s * d + (j + 1) * tc)
            return jnp.dot(h_scr[...], w_ref[:, cols], preferred_element_type=F32) + b_ref[:, cols]

        def put(ref, val):
            for g in range(tc // LANES):
                rows = _group_rows(j * (tc // LANES) + g, tm, groups)
                ref[rows, :] = val[:, g * LANES:(g + 1) * LANES]

        put(q_ref, _silu(seg(0)))
        lg = lbl_ref[:, j * tc:(j + 1) * tc]
        e = jnp.exp(lg - jnp.max(lg, axis=0, keepdims=True))
        lb = jnp.sum(e[:layer + 1], axis=0, keepdims=True) / jnp.sum(e, axis=0, keepdims=True)
        put(f_ref, lb + (1.0 - lb) * jax.nn.sigmoid(seg(1)))
        put(v_ref, seg(2))
        og_act = _silu(seg(3))
        ga_ref[:, j * tc:(j + 1) * tc] = og_act * jax.nn.sigmoid(seg(6))
        glu_a = seg(4)
        put(u_ref, glu_a * jax.nn.sigmoid(seg(5)))
        sgb_ref[:, j * tc:(j + 1) * tc] = jax.nn.sigmoid(seg(7)).astype(sgb_ref.dtype)


def _mod_spec(mod, row, d, tm, rows_per_mod):
    tiles_per_mod = rows_per_mod // tm
    return pl.BlockSpec((None, mod.shape[1], d), lambda i: (i // tiles_per_mod, 0, row))


def _inproj(x, mod, nw, w, b, lbl, *, layer, tm, tc, rows_per_mod):
    m, d = x.shape
    groups = d // LANES
    mod_spec = lambda row: _mod_spec(mod, row, d, tm, rows_per_mod)
    row_spec = pl.BlockSpec((tm, d), lambda i: (i, 0))
    il_spec = pl.BlockSpec((tm * groups, LANES), lambda i: (i, 0))
    il_shape = jax.ShapeDtypeStruct((m * groups, LANES), F32)
    return pl.pallas_call(
        functools.partial(_inproj_kernel, layer=layer, tc=tc),
        grid=(m // tm,),
        in_specs=[row_spec, mod_spec(0), mod_spec(1),
                  _const_spec((1, d)), _const_spec(w.shape), _const_spec(b.shape),
                  _const_spec(lbl.shape)],
        out_specs=[il_spec, il_spec, il_spec, row_spec, il_spec, row_spec],
        out_shape=[il_shape, il_shape, il_shape, jax.ShapeDtypeStruct((m, d), F32), il_shape,
                   jax.ShapeDtypeStruct((m, d), BF16)],
        scratch_shapes=[pltpu.VMEM((tm, d), BF16)],
        compiler_params=_params("parallel"),
        name="inproj",
    )(x, mod, mod, nw, w, b, lbl)


def _split_level_table(c):
    t = np.arange(c)[:, None]
    s = np.arange(c)[None, :]
    x = t ^ s
    lvl = np.zeros((c, c), np.int32)
    for l in range(1, c.bit_length()):
        lvl[(x >> (l - 1)) == 1] = l
    return np.where(t > s, lvl, 0).astype(np.int32)


def _recur_kernel(q_ref, f_ref, v_ref, ga_ref, nw_ref, lvl_ref, o_ref, s_out_ref,
                  st_scr, x_scr, d_scr, *, heads, chunk):
    c_idx = pl.program_id(1)
    nlev = chunk.bit_length() - 1

    @pl.when(c_idx == 0)
    def _():
        st_scr[...] = jnp.zeros_like(st_scr)

    def to3(a):
        return a.reshape(chunk, heads, HEAD_DIM)

    def flat(a):
        return a.reshape(chunk * heads, HEAD_DIM)

    lvl = lvl_ref[...]
    rows_chunk = chunk * heads

    def chunk_body(ci, carry):
        base = pl.multiple_of(ci * rows_chunk, rows_chunk)
        whole = pl.ds(base, rows_chunk)
        steps = pl.ds(pl.multiple_of(ci * chunk, chunk), chunk)
        f3 = to3(f_ref[whole, :])
        q3 = to3(q_ref[whole, :])
        v3 = to3(v_ref[whole, :])
        k3 = 1.0 - f3
        d_scr[...] = flat(jnp.sum(q3 * k3, axis=-1, keepdims=True) * v3)
        qp, kr, tot = q3 * f3, k3, f3
        half = 1
        for l in range(nlev):
            blk = 2 * half
            shp = (chunk // blk, blk, heads, HEAD_DIM)
            qp4, kr4 = qp.reshape(shp), kr.reshape(shp)
            x_scr[l] = flat(jnp.concatenate([kr4[:, :half], qp4[:, half:]], axis=1))
            tot2 = tot.reshape(chunk // blk, 2, heads, HEAD_DIM)
            left_tot, right_tot = tot2[:, 0:1], tot2[:, 1:2]
            qp = jnp.concatenate([qp4[:, :half], qp4[:, half:] * left_tot], axis=1).reshape(f3.shape)
            kr = jnp.concatenate([kr4[:, :half] * right_tot, kr4[:, half:]], axis=1).reshape(f3.shape)
            tot = (left_tot * right_tot).reshape(chunk // blk, heads, HEAD_DIM)
            half = blk
        x_scr[nlev] = flat(qp)
        x_scr[nlev + 1] = flat(kr)
        p_last = tot[0]

        for h in range(heads):
            rows = _group_rows(h, chunk, heads)
            blk_rows = _group_rows(base + h, chunk, heads)
            att = jnp.zeros((chunk, chunk), F32)
            for l in range(nlev):
                xb = x_scr[l, rows, :].astype(BF16)
                a = lax.dot_general(xb, xb, NT, preferred_element_type=F32)
                att = jnp.where(lvl == l + 1, a, att)
            qt = x_scr[nlev, rows, :].astype(BF16)
            kt = x_scr[nlev + 1, rows, :].astype(BF16)
            vb = v_ref[blk_rows, :].astype(BF16)
            st0 = st_scr[h]
            o = lax.dot_general(qt, st0.astype(BF16), NT, preferred_element_type=F32)
            o = o + jnp.dot(att.astype(BF16), vb, preferred_element_type=F32) + d_scr[rows, :]
            st_scr[h] = st0 * p_last[h:h + 1, :] + lax.dot_general(vb, kt, TN,
                                                                   preferred_element_type=F32)
            o = o * lax.rsqrt(jnp.mean(o * o, axis=-1, keepdims=True) + EPS)
            cols = slice(h * HEAD_DIM, (h + 1) * HEAD_DIM)
            o_ref[steps, cols] = (o * nw_ref[h:h + 1, :] * ga_ref[steps, cols]).astype(o_ref.dtype)
        return carry

    lax.fori_loop(0, q_ref.shape[0] // rows_chunk, chunk_body, 0)

    @pl.when(c_idx == pl.num_programs(1) - 1)
    def _():
        for h in range(heads):
            s_out_ref[h] = st_scr[h].T


def _recur(q, f, v, ga, norm_w, *, n, heads, chunk, chunks_per_step):
    rows_total = q.shape[0]
    l = rows_total // (n * heads)
    nlev = chunk.bit_length() - 1
    rows = chunk * heads
    steps = l // (chunk * chunks_per_step)
    blk = pl.BlockSpec((chunks_per_step * rows, HEAD_DIM), lambda i, c: (i * steps + c, 0))
    row_blk = pl.BlockSpec((chunks_per_step * chunk, heads * HEAD_DIM), lambda i, c: (i * steps + c, 0))
    st_spec = pl.BlockSpec((None, None, heads, HEAD_DIM, HEAD_DIM), lambda i, c: (0, i, 0, 0, 0))
    lvl = jnp.asarray(_split_level_table(chunk))
    return pl.pallas_call(
        functools.partial(_recur_kernel, heads=heads, chunk=chunk),
        grid=(n, steps),
        in_specs=[blk, blk, blk, row_blk, _const_spec((heads, HEAD_DIM)), _const_spec((chunk, chunk))],
        out_specs=[row_blk, st_spec],
        out_shape=[jax.ShapeDtypeStruct(ga.shape, BF16),
                   jax.ShapeDtypeStruct((1, n, heads, HEAD_DIM, HEAD_DIM), F32)],
        scratch_shapes=[pltpu.VMEM((heads, HEAD_DIM, HEAD_DIM), F32),
                        pltpu.VMEM((nlev + 2, rows, HEAD_DIM), F32),
                        pltpu.VMEM((rows, HEAD_DIM), F32)],
        compiler_params=_params("parallel", "arbitrary"),
        name="recur",
    )(q, f, v, ga, norm_w.reshape(heads, HEAD_DIM), lvl)


def _recur_step_kernel(q_ref, f_ref, v_ref, ga_ref, nw_ref, s0_ref, o_ref, s_out_ref,
                       qt_scr, kt_scr, vp_scr, oi_scr, pt_scr, g_scr, *, heads, steps, seqs, pad):
    rows_seq = steps * heads
    zero = jnp.zeros((pad * heads, HEAD_DIM), F32)
    qt_scr[...] = zero
    kt_scr[...] = zero
    vp_scr[...] = zero
    pt_scr[...] = jnp.zeros_like(pt_scr)

    def regroup(t, h):
        return pl.ds(t * heads + h, seqs, stride=rows_seq)

    for t in range(steps):
        for h in range(heads):
            g_scr[regroup(t, h), :] = ga_ref[t, :, h * HEAD_DIM:(h + 1) * HEAD_DIM]

    def seq_body(g, carry):
        base = pl.multiple_of(g * rows_seq, rows_seq)
        blk = pl.ds(base, rows_seq)
        seq = pl.ds(pl.multiple_of(g * heads, heads), heads)
        to3 = lambda a: a.reshape(steps, heads, HEAD_DIM)
        q3, f3, v3 = (jnp.stack([ref[t, seq, :] for t in range(steps)]) for ref in (q_ref, f_ref, v_ref))
        k3 = 1.0 - f3
        p = [f3[0]]
        for t in range(1, steps):
            p.append(p[-1] * f3[t])
        r = [None] * steps
        r[steps - 1] = jnp.ones_like(f3[0])
        for s in range(steps - 2, -1, -1):
            r[s] = r[s + 1] * f3[s + 1]
        intra = []
        for t in range(steps):
            acc = jnp.sum(q3[t] * k3[t], axis=-1, keepdims=True) * v3[t]
            dec = None
            for s in range(t - 1, -1, -1):
                dec = f3[s + 1] if dec is None else dec * f3[s + 1]
                acc = acc + jnp.sum(q3[t] * k3[s] * dec, axis=-1, keepdims=True) * v3[s]
            intra.append(acc)
        for t in range(steps):
            rows_t = pl.ds(t * heads, heads)
            qt_scr[rows_t, :] = q3[t] * p[t]
            kt_scr[rows_t, :] = k3[t] * r[t]
            vp_scr[rows_t, :] = v3[t]
        pt_scr[0:heads, :] = p[steps - 1]
        p_col = pt_scr[...].T
        for h in range(heads):
            rows = _group_rows(h, pad, heads)
            qt = qt_scr[rows, :].astype(BF16)
            kt = kt_scr[rows, :].astype(BF16)
            vb = vp_scr[rows, :].astype(BF16)
            s0 = s0_ref[g, h]
            oi_scr[rows, :] = jnp.dot(qt, s0.astype(BF16), preferred_element_type=F32)
            s_out_ref[g, h] = s0 * p_col[:, h:h + 1] + lax.dot_general(
                kt, vb, TN, preferred_element_type=F32)
        o3 = to3(oi_scr[0:rows_seq, :]) + jnp.stack(intra, axis=0)
        o3 = o3 * lax.rsqrt(jnp.mean(o3 * o3, axis=-1, keepdims=True) + EPS)
        g_scr[blk, :] = (o3 * nw_ref[...] * to3(g_scr[blk, :])).reshape(rows_seq, HEAD_DIM)
        return carry

    lax.fori_loop(0, seqs, seq_body, 0)
    for t in range(steps):
        for h in range(heads):
            o_ref[t, :, h * HEAD_DIM:(h + 1) * HEAD_DIM] = g_scr[regroup(t, h), :].astype(o_ref.dtype)


def _recur_step(q, f, v, ga, norm_w, states, *, layer, heads, steps, seqs, pad):
    n = states.shape[1]
    rows = seqs * steps * heads
    q, f, v = (a.reshape(steps, n * heads, HEAD_DIM) for a in (q, f, v))
    ga = ga.reshape(steps, n, heads * HEAD_DIM)
    blk = pl.BlockSpec((steps, seqs * heads, HEAD_DIM), lambda i: (0, i, 0))
    row_blk = pl.BlockSpec((steps, seqs, heads * HEAD_DIM), lambda i: (0, i, 0))
    st_blk = (None, seqs, heads, HEAD_DIM, HEAD_DIM)
    scr = pltpu.VMEM((pad * heads, HEAD_DIM), F32)
    o, s_new = pl.pallas_call(
        functools.partial(_recur_step_kernel, heads=heads, steps=steps, seqs=seqs, pad=pad),
        grid=(n // seqs,),
        in_specs=[blk, blk, blk, row_blk, _const_spec((heads, HEAD_DIM)),
                  pl.BlockSpec(st_blk, lambda i: (layer, i, 0, 0, 0))],
        out_specs=[row_blk, pl.BlockSpec(st_blk, lambda i: (0, i, 0, 0, 0))],
        out_shape=[jax.ShapeDtypeStruct(ga.shape, BF16), jax.ShapeDtypeStruct((1,) + states.shape[1:], F32)],
        scratch_shapes=[scr, scr, scr, scr, pltpu.VMEM((HEAD_DIM, HEAD_DIM), F32),
                        pltpu.VMEM((rows, HEAD_DIM), F32)],
        compiler_params=_params("parallel"),
        name="recur_step",
    )(q, f, v, ga, norm_w.reshape(heads, HEAD_DIM), states)
    return o.reshape(steps * n, heads * HEAD_DIM), s_new


def _conv_taps(ext_scr, w_ref, b_ref, y_ref, *, ext_row0, out_row0, tb, groups):
    acc = [jnp.broadcast_to(b_ref[...], (tb, groups, LANES)), jnp.zeros((tb, groups, LANES), F32)]
    for j in range(CONV_WIDTH):
        start = (ext_row0 + j) * groups
        if not isinstance(start, int):
            start = pl.multiple_of(start, groups)
        e = ext_scr[pl.ds(start, tb * groups), :].reshape(tb, groups, LANES)
        acc[j % 2] = acc[j % 2] + w_ref[j] * e
    y_ref[pl.ds(out_row0 * groups, tb * groups), :] = (acc[0] + acc[1]).reshape(tb * groups, LANES)


def _conv_kernel(u_ref, w_ref, b_ref, y_ref, buf_out_ref, ext_scr, *, tile, tb, groups):
    t_idx = pl.program_id(1)
    hist = CONV_HIST * groups

    @pl.when(t_idx == 0)
    def _():
        ext_scr[0:hist, :] = jnp.zeros((hist, LANES), F32)

    @pl.when(t_idx > 0)
    def _():
        ext_scr[0:hist, :] = ext_scr[tile * groups:tile * groups + hist, :]

    ext_scr[hist:hist + tile * groups, :] = u_ref[...]

    def block_body(i, carry):
        r0 = pl.multiple_of(i * tb, tb)
        _conv_taps(ext_scr, w_ref, b_ref, y_ref, ext_row0=r0, out_row0=r0, tb=tb, groups=groups)
        return carry

    lax.fori_loop(0, tile // tb, block_body, 0)

    @pl.when(t_idx == pl.num_programs(1) - 1)
    def _():
        for g in range(groups):
            buf_out_ref[:, g * LANES:(g + 1) * LANES] = ext_scr[
                pl.ds(tile * groups + g, CONV_HIST, stride=groups), :]


def _conv(u, dw_w, dw_b, *, n, tile, tb):
    ch = dw_w.shape[1]
    groups = ch // LANES
    l = u.shape[0] // (n * groups)
    steps = l // tile
    blk = pl.BlockSpec((tile * groups, LANES), lambda i, t: (i * steps + t, 0))
    return pl.pallas_call(
        functools.partial(_conv_kernel, tile=tile, tb=tb, groups=groups),
        grid=(n, steps),
        in_specs=[blk, _const_spec((CONV_WIDTH, groups, LANES)), _const_spec((1, groups, LANES))],
        out_specs=[blk, pl.BlockSpec((None, None, CONV_HIST, ch), lambda i, t: (0, i, 0, 0))],
        out_shape=[jax.ShapeDtypeStruct(u.shape, F32), jax.ShapeDtypeStruct((1, n, CONV_HIST, ch), F32)],
        scratch_shapes=[pltpu.VMEM(((tile + CONV_HIST) * groups, LANES), F32)],
        compiler_params=_params("parallel", "arbitrary"),
        name="conv",
    )(u, dw_w.reshape(CONV_WIDTH, groups, LANES), dw_b.reshape(1, groups, LANES))


def _conv_step_kernel(u_ref, w_ref, b_ref, buf_ref, y_ref, buf_out_ref, *, steps, seqs, groups):
    for g in range(groups):
        cols = slice(g * LANES, (g + 1) * LANES)
        rows = _group_rows(g, seqs, groups)
        ext = [buf_ref[t, :, cols] for t in range(CONV_HIST)] + [u_ref[t, rows, :] for t in range(steps)]
        for t in range(steps):
            acc = [jnp.broadcast_to(b_ref[0, g:g + 1, :], (seqs, LANES)), jnp.zeros((seqs, LANES), F32)]
            for j in range(CONV_WIDTH):
                acc[j % 2] = acc[j % 2] + w_ref[j, g:g + 1, :] * ext[t + j]
            y_ref[t, rows, :] = acc[0] + acc[1]
        for t in range(CONV_HIST):
            buf_out_ref[t, :, cols] = ext[t + steps]


def _conv_step(u, dw_w, dw_b, bufs, *, layer, steps, seqs):
    _, _, n, ch = bufs.shape
    groups = ch // LANES
    blk = pl.BlockSpec((steps, seqs * groups, LANES), lambda i: (0, i, 0))
    buf_blk = (None, CONV_HIST, seqs, ch)
    y, buf_new = pl.pallas_call(
        functools.partial(_conv_step_kernel, steps=steps, seqs=seqs, groups=groups),
        grid=(n // seqs,),
        in_specs=[blk, _const_spec((CONV_WIDTH, groups, LANES)), _const_spec((1, groups, LANES)),
                  pl.BlockSpec(buf_blk, lambda i: (layer, 0, i, 0))],
        out_specs=[blk, pl.BlockSpec(buf_blk, lambda i: (0, 0, i, 0))],
        out_shape=[jax.ShapeDtypeStruct((steps, n * groups, LANES), F32),
                   jax.ShapeDtypeStruct((1,) + bufs.shape[1:], F32)],
        compiler_params=_params("parallel"),
        name="conv_step",
    )(u.reshape(steps, n * groups, LANES), dw_w.reshape(CONV_WIDTH, groups, LANES),
      dw_b.reshape(1, groups, LANES), bufs)
    return y.reshape(u.shape), buf_new


def _post_kernel(x_ref, oa_ref, yc_ref, sgb_ref, g1_ref, sh2_ref, sc2_ref, g2_ref, lnw_ref, lnb_ref,
                 wo_ref, nfw_ref, w1_ref, w2_ref, nlw_ref, y_ref, c_scr, h_scr, acc_scr, *, fc):
    tm, d = x_ref.shape
    groups = d // LANES
    for g in range(groups):
        c_scr[:, g * LANES:(g + 1) * LANES] = yc_ref[_group_rows(g, tm, groups), :]
    yc = c_scr[...]
    mu = jnp.mean(yc, axis=-1, keepdims=True)
    dv = yc - mu
    var = jnp.mean(dv * dv, axis=-1, keepdims=True)
    ob = _silu(dv * lax.rsqrt(var + EPS) * lnw_ref[...] + lnb_ref[...]) * sgb_ref[...].astype(F32)
    merged = jnp.dot(oa_ref[...], wo_ref[0:d, :], preferred_element_type=F32)
    merged = merged + jnp.dot(ob.astype(BF16), wo_ref[d:2 * d, :], preferred_element_type=F32)
    x1 = x_ref[...] + _mod_rows(g1_ref, tm) * merged
    ms = jnp.mean(x1 * x1, axis=-1, keepdims=True)
    h2 = x1 * lax.rsqrt(ms + EPS) * nfw_ref[...]
    h2 = h2 * (1.0 + _mod_rows(sc2_ref, tm)) + _mod_rows(sh2_ref, tm)
    h_scr[...] = h2.astype(BF16)
    d_ff = w2_ref.shape[0]
    for c in range(d_ff // fc):
        gate = jnp.dot(h_scr[...], w1_ref[:, c * fc:(c + 1) * fc], preferred_element_type=F32)
        up = jnp.dot(h_scr[...], w1_ref[:, d_ff + c * fc:d_ff + (c + 1) * fc], preferred_element_type=F32)
        part = jnp.dot((_silu(gate) * up).astype(BF16), w2_ref[c * fc:(c + 1) * fc, :],
                       preferred_element_type=F32)
        if c == 0:
            acc_scr[...] = part
        else:
            acc_scr[...] += part
    x2 = x1 + _mod_rows(g2_ref, tm) * acc_scr[...]
    ms2 = jnp.mean(x2 * x2, axis=-1, keepdims=True)
    y_ref[...] = x2 * lax.rsqrt(ms2 + EPS) * nlw_ref[...]


def _post(x, oa, yc, sgb, mod, lnw, lnb, wo, nfw, w1, w2, nlw, *, tm, fc, rows_per_mod):
    m, d = x.shape
    groups = d // LANES
    mod_spec = lambda row: _mod_spec(mod, row, d, tm, rows_per_mod)
    row_spec = pl.BlockSpec((tm, d), lambda i: (i, 0))
    il_spec = pl.BlockSpec((tm * groups, LANES), lambda i: (i, 0))
    vec = _const_spec((1, d))
    return pl.pallas_call(
        functools.partial(_post_kernel, fc=fc),
        grid=(m // tm,),
        in_specs=[row_spec, row_spec, il_spec, row_spec, mod_spec(2), mod_spec(3), mod_spec(4), mod_spec(5),
                  vec, vec, _const_spec(wo.shape), vec, _const_spec(w1.shape), _const_spec(w2.shape), vec],
        out_specs=row_spec,
        out_shape=jax.ShapeDtypeStruct((m, d), F32),
        scratch_shapes=[pltpu.VMEM((tm, d), F32), pltpu.VMEM((tm, d), BF16), pltpu.VMEM((tm, d), F32)],
        compiler_params=_params("parallel"),
        name="post",
    )(x, oa, yc, sgb, mod, mod, mod, mod, lnw, lnb, wo, nfw, w1, w2, nlw)


INPROJ_TC = 256
FFN_CHUNK = 256
MOD_TN = 1024
PROMPT_TM = 512
PROMPT_CHUNK = 128
PROMPT_CHUNKS_PER_STEP = 4
PROMPT_CONV_TILE = 512
PROMPT_CONV_TB = 16
SAMPLE_SEQS = 16
SAMPLE_PAD = 16


def kernel(x_prompt, x_sample, state_hgrn, state_conv, c_prompt, c_sample, w_ada, b_ada, norm_mix_w, w_in, b_in, lb_logits, hgrn_norm_w, conv_dw_w, conv_dw_b, conv_ln_w, conv_ln_b, w_out, norm_ffn_w, w_ffn_in, w_ffn_out, norm_final_w):
    depth = w_in.shape[0]
    assert depth == 1, "single-layer configuration"
    layer = 0
    n_p, l_p, d = x_prompt.shape
    n_s, l_s, _ = x_sample.shape
    heads = d // HEAD_DIM

    w_in_b = w_in[layer].astype(BF16)
    w1 = w_ffn_in[layer].astype(BF16)
    w2 = w_ffn_out[layer].astype(BF16)
    wo = w_out[layer].astype(BF16)
    row = lambda a: a.reshape(1, -1)

    mod = _mod(jnp.concatenate([c_prompt, c_sample], axis=0), w_ada[layer], b_ada[layer], tn=MOD_TN)
    mod_p = mod[:n_p].reshape(n_p, 1, MOD_ROWS * d)
    mod_s = mod[n_p:].reshape(1, n_s, MOD_ROWS * d)

    def inproj(x2, mod_g, tm):
        return _inproj(x2, mod_g, row(norm_mix_w[layer]), w_in_b, row(b_in[layer]), lb_logits,
                       layer=layer, tm=tm, tc=INPROJ_TC, rows_per_mod=tm if mod_g.shape[1] > 1 else l_p)

    def post(x2, oa, yc, sgb, mod_g, tm):
        return _post(x2, oa, yc, sgb, mod_g, row(conv_ln_w[layer]), row(conv_ln_b[layer]), wo,
                     row(norm_ffn_w[layer]), w1, w2, row(norm_final_w),
                     tm=tm, fc=FFN_CHUNK, rows_per_mod=tm if mod_g.shape[1] > 1 else l_p)

    xp = x_prompt.reshape(n_p * l_p, d)
    q, f, v, ga, u, sgb = inproj(xp, mod_p, PROMPT_TM)
    oa, s_p = _recur(q, f, v, ga, hgrn_norm_w[layer], n=n_p, heads=heads, chunk=PROMPT_CHUNK,
                     chunks_per_step=PROMPT_CHUNKS_PER_STEP)
    yc, b_p = _conv(u, conv_dw_w[layer], conv_dw_b[layer], n=n_p, tile=PROMPT_CONV_TILE, tb=PROMPT_CONV_TB)
    y_p = post(xp, oa, yc, sgb, mod_p, PROMPT_TM).reshape(n_p, l_p, d)

    xs = x_sample.transpose(1, 0, 2).reshape(l_s * n_s, d)
    q, f, v, ga, u, sgb = inproj(xs, mod_s, n_s * l_s)
    oa, s_s = _recur_step(q, f, v, ga, hgrn_norm_w[layer], state_hgrn, layer=layer,
                          heads=heads, steps=l_s, seqs=SAMPLE_SEQS, pad=SAMPLE_PAD)
    yc, b_s = _conv_step(u, conv_dw_w[layer], conv_dw_b[layer], state_conv.transpose(0, 2, 1, 3),
                         layer=layer, steps=l_s, seqs=SAMPLE_SEQS)
    y_s = post(xs, oa, yc, sgb, mod_s, n_s * l_s).reshape(l_s, n_s, d).transpose(1, 0, 2)

    return (y_p, y_s, s_p, b_p, s_s, b_s.transpose(0, 2, 1, 3))
```

```python
import functools

import numpy as np
import jax
import jax.numpy as jnp
from jax import lax
from jax.experimental import pallas as pl
from jax.experimental.pallas import tpu as pltpu

F32 = jnp.float32
BF16 = jnp.bfloat16

EPS = 1e-6
LANES = 128
HEAD_DIM = LANES
CONV_WIDTH = 31
CONV_HIST = CONV_WIDTH - 1
VMEM_LIMIT_BYTES = 56 * 1024 * 1024

MOD_ROWS = 6

NT = (((1,), (1,)), ((), ()))
TN = (((0,), (0,)), ((), ()))


def _silu(x):
    return x * jax.nn.sigmoid(x)


def _params(*sem):
    return pltpu.CompilerParams(dimension_semantics=sem, vmem_limit_bytes=VMEM_LIMIT_BYTES)


def _const_spec(shape):
    nd = len(shape)
    return pl.BlockSpec(shape, lambda *_: (0,) * nd, pipeline_mode=pl.Buffered(1))


def _mod_rows(ref, tm):
    m = ref[...]
    r = m.shape[0]
    return m if r in (1, tm) else jnp.tile(m, (tm // r, 1))


def _group_rows(g, n, groups):
    return pl.ds(g, n, stride=groups)


def _mod_kernel(c_ref, w_ref, b_ref, o_ref):
    a = _silu(c_ref[...]).astype(BF16)
    o_ref[...] = jnp.dot(a, w_ref[...].astype(BF16), preferred_element_type=F32) + b_ref[...]


def _mod(c, w_ada, b_ada, *, tn):
    n, d = c.shape
    cols = w_ada.shape[1]
    return pl.pallas_call(
        _mod_kernel,
        grid=(cols // tn,),
        in_specs=[pl.BlockSpec((n, d), lambda j: (0, 0)),
                  pl.BlockSpec((d, tn), lambda j: (0, j)),
                  pl.BlockSpec((1, tn), lambda j: (0, j))],
        out_specs=pl.BlockSpec((n, tn), lambda j: (0, j)),
        out_shape=jax.ShapeDtypeStruct((n, cols), F32),
        compiler_params=_params("parallel"),
        name="mod",
    )(c, w_ada, b_ada.reshape(1, cols))


def _inproj_kernel(x_ref, sh_ref, sc_ref, nw_ref, w_ref, b_ref, lbl_ref,
                   q_ref, f_ref, v_ref, ga_ref, u_ref, sgb_ref, h_scr, *, layer, tc):
    tm, d = x_ref.shape
    groups = d // LANES
    x = x_ref[...]
    ms = jnp.mean(x * x, axis=-1, keepdims=True)
    h = x * lax.rsqrt(ms + EPS) * nw_ref[...]
    h = h * (1.0 + _mod_rows(sc_ref, tm)) + _mod_rows(sh_ref, tm)
    h_scr[...] = h.astype(BF16)
    for j in range(d // tc):
        def seg(s):
            cols = slice(s * d + j * tc, s * d + (j + 1) * tc)
            return jnp.dot(h_scr[...], w_ref[:, cols], preferred_element_type=F32) + b_ref[:, cols]

        def put(ref, val):
            for g in range(tc // LANES):
                rows = _group_rows(j * (tc // LANES) + g, tm, groups)
                ref[rows, :] = val[:, g * LANES:(g + 1) * LANES]

        put(q_ref, _silu(seg(0)))
        lg = lbl_ref[:, j * tc:(j + 1) * tc]
        e = jnp.exp(lg - jnp.max(lg, axis=0, keepdims=True))
        lb = jnp.sum(e[:layer + 1], axis=0, keepdims=True) / jnp.sum(e, axis=0, keepdims=True)
        put(f_ref, lb + (1.0 - lb) * jax.nn.sigmoid(seg(1)))
        put(v_ref, seg(2))
        og_act = _silu(seg(3))
        ga_ref[:, j * tc:(j + 1) * tc] = og_act * jax.nn.sigmoid(seg(6))
        glu_a = seg(4)
        put(u_ref, glu_a * jax.nn.sigmoid(seg(5)))
        sgb_ref[:, j * tc:(j + 1) * tc] = jax.nn.sigmoid(seg(7)).astype(sgb_ref.dtype)


def _mod_spec(mod, row, d, tm, rows_per_mod):
    tiles_per_mod = rows_per_mod // tm
    return pl.BlockSpec((None, mod.shape[1], d), lambda i: (i // tiles_per_mod, 0, row))


def _inproj(x, mod, nw, w, b, lbl, *, layer, tm, tc, rows_per_mod):
    m, d = x.shape
    groups = d // LANES
    mod_spec = lambda row: _mod_spec(mod, row, d, tm, rows_per_mod)
    row_spec = pl.BlockSpec((tm, d), lambda i: (i, 0))
    il_spec = pl.BlockSpec((tm * groups, LANES), lambda i: (i, 0))
    il_shape = jax.ShapeDtypeStruct((m * groups, LANES), F32)
    return pl.pallas_call(
        functools.partial(_inproj_kernel, layer=layer, tc=tc),
        grid=(m // tm,),
        in_specs=[row_spec, mod_spec(0), mod_spec(1),
                  _const_spec((1, d)), _const_spec(w.shape), _const_spec(b.shape),
                  _const_spec(lbl.shape)],
        out_specs=[il_spec, il_spec, il_spec, row_spec, il_spec, row_spec],
        out_shape=[il_shape, il_shape, il_shape, jax.ShapeDtypeStruct((m, d), F32), il_shape,
                   jax.ShapeDtypeStruct((m, d), BF16)],
        scratch_shapes=[pltpu.VMEM((tm, d), BF16)],
        compiler_params=_params("parallel"),
        name="inproj",
    )(x, mod, mod, nw, w, b, lbl)


def _split_level_table(c):
    t = np.arange(c)[:, None]
    s = np.arange(c)[None, :]
    x = t ^ s
    lvl = np.zeros((c, c), np.int32)
    for l in range(1, c.bit_length()):
        lvl[(x >> (l - 1)) == 1] = l
    return np.where(t > s, lvl, 0).astype(np.int32)


def _recur_kernel(q_ref, f_ref, v_ref, ga_ref, nw_ref, lvl_ref, o_ref, s_out_ref,
                  st_scr, x_scr, d_scr, *, heads, chunk):
    c_idx = pl.program_id(1)
    nlev = chunk.bit_length() - 1

    @pl.when(c_idx == 0)
    def _():
        st_scr[...] = jnp.zeros_like(st_scr)

    def to3(a):
        return a.reshape(chunk, heads, HEAD_DIM)

    def flat(a):
        return a.reshape(chunk * heads, HEAD_DIM)

    lvl = lvl_ref[...]
    rows_chunk = chunk * heads

    def chunk_body(ci, carry):
        base = pl.multiple_of(ci * rows_chunk, rows_chunk)
        whole = pl.ds(base, rows_chunk)
        steps = pl.ds(pl.multiple_of(ci * chunk, chunk), chunk)
        f3 = to3(f_ref[whole, :])
        q3 = to3(q_ref[whole, :])
        v3 = to3(v_ref[whole, :])
        k3 = 1.0 - f3
        d_scr[...] = flat(jnp.sum(q3 * k3, axis=-1, keepdims=True) * v3)
        qp, kr, tot = q3 * f3, k3, f3
        half = 1
        for l in range(nlev):
            blk = 2 * half
            shp = (chunk // blk, blk, heads, HEAD_DIM)
            qp4, kr4 = qp.reshape(shp), kr.reshape(shp)
            x_scr[l] = flat(jnp.concatenate([kr4[:, :half], qp4[:, half:]], axis=1))
            tot2 = tot.reshape(chunk // blk, 2, heads, HEAD_DIM)
            left_tot, right_tot = tot2[:, 0:1], tot2[:, 1:2]
            qp = jnp.concatenate([qp4[:, :half], qp4[:, half:] * left_tot], axis=1).reshape(f3.shape)
            kr = jnp.concatenate([kr4[:, :half] * right_tot, kr4[:, half:]], axis=1).reshape(f3.shape)
            tot = (left_tot * right_tot).reshape(chunk // blk, heads, HEAD_DIM)
            half = blk
        x_scr[nlev] = flat(qp)
        x_scr[nlev + 1] = flat(kr)
        p_last = tot[0]

        for h in range(heads):
            rows = _group_rows(h, chunk, heads)
            blk_rows = _group_rows(base + h, chunk, heads)
            att = jnp.zeros((chunk, chunk), F32)
            for l in range(nlev):
                xb = x_scr[l, rows, :].astype(BF16)
                a = lax.dot_general(xb, xb, NT, preferred_element_type=F32)
                att = jnp.where(lvl == l + 1, a, att)
            qt = x_scr[nlev, rows, :].astype(BF16)
            kt = x_scr[nlev + 1, rows, :].astype(BF16)
            vb = v_ref[blk_rows, :].astype(BF16)
            st0 = st_scr[h]
            o = lax.dot_general(qt, st0.astype(BF16), NT, preferred_element_type=F32)
            o = o + jnp.dot(att.astype(BF16), vb, preferred_element_type=F32) + d_scr[rows, :]
            st_scr[h] = st0 * p_last[h:h + 1, :] + lax.dot_general(vb, kt, TN,
                                                                   preferred_element_type=F32)
            o = o * lax.rsqrt(jnp.mean(o * o, axis=-1, keepdims=True) + EPS)
            cols = slice(h * HEAD_DIM, (h + 1) * HEAD_DIM)
            o_ref[steps, cols] = (o * nw_ref[h:h + 1, :] * ga_ref[steps, cols]).astype(o_ref.dtype)
        return carry

    lax.fori_loop(0, q_ref.shape[0] // rows_chunk, chunk_body, 0)

    @pl.when(c_idx == pl.num_programs(1) - 1)
    def _():
        for h in range(heads):
            s_out_ref[h] = st_scr[h].T


def _recur(q, f, v, ga, norm_w, *, n, heads, chunk, chunks_per_step):
    rows_total = q.shape[0]
    l = rows_total // (n * heads)
    nlev = chunk.bit_length() - 1
    rows = chunk * heads
    steps = l // (chunk * chunks_per_step)
    blk = pl.BlockSpec((chunks_per_step * rows, HEAD_DIM), lambda i, c: (i * steps + c, 0))
    row_blk = pl.BlockSpec((chunks_per_step * chunk, heads * HEAD_DIM), lambda i, c: (i * steps + c, 0))
    st_spec = pl.BlockSpec((None, None, heads, HEAD_DIM, HEAD_DIM), lambda i, c: (0, i, 0, 0, 0))
    lvl = jnp.asarray(_split_level_table(chunk))
    return pl.pallas_call(
        functools.partial(_recur_kernel, heads=heads, chunk=chunk),
        grid=(n, steps),
        in_specs=[blk, blk, blk, row_blk, _const_spec((heads, HEAD_DIM)), _const_spec((chunk, chunk))],
        out_specs=[row_blk, st_spec],
        out_shape=[jax.ShapeDtypeStruct(ga.shape, BF16),
                   jax.ShapeDtypeStruct((1, n, heads, HEAD_DIM, HEAD_DIM), F32)],
        scratch_shapes=[pltpu.VMEM((heads, HEAD_DIM, HEAD_DIM), F32),
                        pltpu.VMEM((nlev + 2, rows, HEAD_DIM), F32),
                        pltpu.VMEM((rows, HEAD_DIM), F32)],
        compiler_params=_params("parallel", "arbitrary"),
        name="recur",
    )(q, f, v, ga, norm_w.reshape(heads, HEAD_DIM), lvl)


def _recur_step_kernel(q_ref, f_ref, v_ref, ga_ref, nw_ref, s0_ref, o_ref, s_out_ref,
                       qt_scr, kt_scr, vp_scr, oi_scr, pt_scr, g_scr, *, heads, steps, seqs, pad):
    rows_seq = steps * heads
    zero = jnp.zeros((pad * heads, HEAD_DIM), F32)
    qt_scr[...] = zero
    kt_scr[...] = zero
    vp_scr[...] = zero
    pt_scr[...] = jnp.zeros_like(pt_scr)

    def regroup(t, h):
        return pl.ds(t * heads + h, seqs, stride=rows_seq)

    for t in range(steps):
        for h in range(heads):
            g_scr[regroup(t, h), :] = ga_ref[t, :, h * HEAD_DIM:(h + 1) * HEAD_DIM]

    def seq_body(g, carry):
        base = pl.multiple_of(g * rows_seq, rows_seq)
        blk = pl.ds(base, rows_seq)
        seq = pl.ds(pl.multiple_of(g * heads, heads), heads)
        to3 = lambda a: a.reshape(steps, heads, HEAD_DIM)
        q3, f3, v3 = (jnp.stack([ref[t, seq, :] for t in range(steps)]) for ref in (q_ref, f_ref, v_ref))
        k3 = 1.0 - f3
        p = [f3[0]]
        for t in range(1, steps):
            p.append(p[-1] * f3[t])
        r = [None] * steps
        r[steps - 1] = jnp.ones_like(f3[0])
        for s in range(steps - 2, -1, -1):
            r[s] = r[s + 1] * f3[s + 1]
        intra = []
        for t in range(steps):
            acc = jnp.sum(q3[t] * k3[t], axis=-1, keepdims=True) * v3[t]
            dec = None
            for s in range(t - 1, -1, -1):
                dec = f3[s + 1] if dec is None else dec * f3[s + 1]
                acc = acc + jnp.sum(q3[t] * k3[s] * dec, axis=-1, keepdims=True) * v3[s]
            intra.append(acc)
        for t in range(steps):
            rows_t = pl.ds(t * heads, heads)
            qt_scr[rows_t, :] = q3[t] * p[t]
            kt_scr[rows_t, :] = k3[t] * r[t]
            vp_scr[rows_t, :] = v3[t]
        pt_scr[0:heads, :] = p[steps - 1]
        p_col = pt_scr[...].T
        for h in range(heads):
            rows = _group_rows(h, pad, heads)
            qt = qt_scr[rows, :].astype(BF16)
            kt = kt_scr[rows, :].astype(BF16)
            vb = vp_scr[rows, :].astype(BF16)
            s0 = s0_ref[g, h]
            oi_scr[rows, :] = jnp.dot(qt, s0.astype(BF16), preferred_element_type=F32)
            s_out_ref[g, h] = s0 * p_col[:, h:h + 1] + lax.dot_general(
                kt, vb, TN, preferred_element_type=F32)
        o3 = to3(oi_scr[0:rows_seq, :]) + jnp.stack(intra, axis=0)
        o3 = o3 * lax.rsqrt(jnp.mean(o3 * o3, axis=-1, keepdims=True) + EPS)
        g_scr[blk, :] = (o3 * nw_ref[...] * to3(g_scr[blk, :])).reshape(rows_seq, HEAD_DIM)
        return carry

    lax.fori_loop(0, seqs, seq_body, 0)
    for t in range(steps):
        for h in range(heads):
            o_ref[t, :, h * HEAD_DIM:(h + 1) * HEAD_DIM] = g_scr[regroup(t, h), :].astype(o_ref.dtype)


def _recur_step(q, f, v, ga, norm_w, states, *, layer, heads, steps, seqs, pad):
    n = states.shape[1]
    rows = seqs * steps * heads
    q, f, v = (a.reshape(steps, n * heads, HEAD_DIM) for a in (q, f, v))
    ga = ga.reshape(steps, n, heads * HEAD_DIM)
    blk = pl.BlockSpec((steps, seqs * heads, HEAD_DIM), lambda i: (0, i, 0))
    row_blk = pl.BlockSpec((steps, seqs, heads * HEAD_DIM), lambda i: (0, i, 0))
    st_blk = (None, seqs, heads, HEAD_DIM, HEAD_DIM)
    scr = pltpu.VMEM((pad * heads, HEAD_DIM), F32)
    o, s_new = pl.pallas_call(
        functools.partial(_recur_step_kernel, heads=heads, steps=steps, seqs=seqs, pad=pad),
        grid=(n // seqs,),
        in_specs=[blk, blk, blk, row_blk, _const_spec((heads, HEAD_DIM)),
                  pl.BlockSpec(st_blk, lambda i: (layer, i, 0, 0, 0))],
        out_specs=[row_blk, pl.BlockSpec(st_blk, lambda i: (0, i, 0, 0, 0))],
        out_shape=[jax.ShapeDtypeStruct(ga.shape, BF16), jax.ShapeDtypeStruct((1,) + states.shape[1:], F32)],
        scratch_shapes=[scr, scr, scr, scr, pltpu.VMEM((HEAD_DIM, HEAD_DIM), F32),
                        pltpu.VMEM((rows, HEAD_DIM), F32)],
        compiler_params=_params("parallel"),
        name="recur_step",
    )(q, f, v, ga, norm_w.reshape(heads, HEAD_DIM), states)
    return o.reshape(steps * n, heads * HEAD_DIM), s_new


def _conv_taps(ext_scr, w_ref, b_ref, y_ref, *, ext_row0, out_row0, tb, groups):
    acc = [jnp.broadcast_to(b_ref[...], (tb, groups, LANES)), jnp.zeros((tb, groups, LANES), F32)]
    for j in range(CONV_WIDTH):
        start = (ext_row0 + j) * groups
        if not isinstance(start, int):
            start = pl.multiple_of(start, groups)
        e = ext_scr[pl.ds(start, tb * groups), :].reshape(tb, groups, LANES)
        acc[j % 2] = acc[j % 2] + w_ref[j] * e
    y_ref[pl.ds(out_row0 * groups, tb * groups), :] = (acc[0] + acc[1]).reshape(tb * groups, LANES)


def _conv_kernel(u_ref, w_ref, b_ref, y_ref, buf_out_ref, ext_scr, *, tile, tb, groups):
    t_idx = pl.program_id(1)
    hist = CONV_HIST * groups

    @pl.when(t_idx == 0)
    def _():
        ext_scr[0:hist, :] = jnp.zeros((hist, LANES), F32)

    @pl.when(t_idx > 0)
    def _():
        ext_scr[0:hist, :] = ext_scr[tile * groups:tile * groups + hist, :]

    ext_scr[hist:hist + tile * groups, :] = u_ref[...]

    def block_body(i, carry):
        r0 = pl.multiple_of(i * tb, tb)
        _conv_taps(ext_scr, w_ref, b_ref, y_ref, ext_row0=r0, out_row0=r0, tb=tb, groups=groups)
        return carry

    lax.fori_loop(0, tile // tb, block_body, 0)

    @pl.when(t_idx == pl.num_programs(1) - 1)
    def _():
        for g in range(groups):
            buf_out_ref[:, g * LANES:(g + 1) * LANES] = ext_scr[
                pl.ds(tile * groups + g, CONV_HIST, stride=groups), :]


def _conv(u, dw_w, dw_b, *, n, tile, tb):
    ch = dw_w.shape[1]
    groups = ch // LANES
    l = u.shape[0] // (n * groups)
    steps = l // tile
    blk = pl.BlockSpec((tile * groups, LANES), lambda i, t: (i * steps + t, 0))
    return pl.pallas_call(
        functools.partial(_conv_kernel, tile=tile, tb=tb, groups=groups),
        grid=(n, steps),
        in_specs=[blk, _const_spec((CONV_WIDTH, groups, LANES)), _const_spec((1, groups, LANES))],
        out_specs=[blk, pl.BlockSpec((None, None, CONV_HIST, ch), lambda i, t: (0, i, 0, 0))],
        out_shape=[jax.ShapeDtypeStruct(u.shape, F32), jax.ShapeDtypeStruct((1, n, CONV_HIST, ch), F32)],
        scratch_shapes=[pltpu.VMEM(((tile + CONV_HIST) * groups, LANES), F32)],
        compiler_params=_params("parallel", "arbitrary"),
        name="conv",
    )(u, dw_w.reshape(CONV_WIDTH, groups, LANES), dw_b.reshape(1, groups, LANES))


def _conv_step_kernel(u_ref, w_ref, b_ref, buf_ref, y_ref, buf_out_ref, *, steps, seqs, groups):
    for g in range(groups):
        cols = slice(g * LANES, (g + 1) * LANES)
        rows = _group_rows(g, seqs, groups)
        ext = [buf_ref[t, :, cols] for t in range(CONV_HIST)] + [u_ref[t, rows, :] for t in range(steps)]
        for t in range(steps):
            acc = [jnp.broadcast_to(b_ref[0, g:g + 1, :], (seqs, LANES)), jnp.zeros((seqs, LANES), F32)]
            for j in range(CONV_WIDTH):
                acc[j % 2] = acc[j % 2] + w_ref[j, g:g + 1, :] * ext[t + j]
            y_ref[t, rows, :] = acc[0] + acc[1]
        for t in range(CONV_HIST):
            buf_out_ref[t, :, cols] = ext[t + steps]


def _conv_step(u, dw_w, dw_b, bufs, *, layer, steps, seqs):
    _, _, n, ch = bufs.shape
    groups = ch // LANES
    blk = pl.BlockSpec((steps, seqs * groups, LANES), lambda i: (0, i, 0))
    buf_blk = (None, CONV_HIST, seqs, ch)
    y, buf_new = pl.pallas_call(
        functools.partial(_conv_step_kernel, steps=steps, seqs=seqs, groups=groups),
        grid=(n // seqs,),
        in_specs=[blk, _const_spec((CONV_WIDTH, groups, LANES)), _const_spec((1, groups, LANES)),
                  pl.BlockSpec(buf_blk, lambda i: (layer, 0, i, 0))],
        out_specs=[blk, pl.BlockSpec(buf_blk, lambda i: (0, 0, i, 0))],
        out_shape=[jax.ShapeDtypeStruct((steps, n * groups, LANES), F32),
                   jax.ShapeDtypeStruct((1,) + bufs.shape[1:], F32)],
        compiler_params=_params("parallel"),
        name="conv_step",
    )(u.reshape(steps, n * groups, LANES), dw_w.reshape(CONV_WIDTH, groups, LANES),
      dw_b.reshape(1, groups, LANES), bufs)
    return y.reshape(u.shape), buf_new


def _post_kernel(x_ref, oa_ref, yc_ref, sgb_ref, g1_ref, sh2_ref, sc2_ref, g2_ref, lnw_ref, lnb_ref,
                 wo_ref, nfw_ref, w1_ref, w2_ref, nlw_ref, y_ref, c_scr, h_scr, act_scr, *, fc):
    tm, d = x_ref.shape
    groups = d // LANES
    for g in range(groups):
        c_scr[:, g * LANES:(g + 1) * LANES] = yc_ref[_group_rows(g, tm, groups), :]
    yc = c_scr[...]
    mu = jnp.mean(yc, axis=-1, keepdims=True)
    dv = yc - mu
    var = jnp.mean(dv * dv, axis=-1, keepdims=True)
    ob = _silu(dv * lax.rsqrt(var + EPS) * lnw_ref[...] + lnb_ref[...]) * sgb_ref[...].astype(F32)
    merged = jnp.dot(oa_ref[...], wo_ref[0:d, :], preferred_element_type=F32)
    merged = merged + jnp.dot(ob.astype(BF16), wo_ref[d:2 * d, :], preferred_element_type=F32)
    x1 = x_ref[...] + _mod_rows(g1_ref, tm) * merged
    ms = jnp.mean(x1 * x1, axis=-1, keepdims=True)
    h2 = x1 * lax.rsqrt(ms + EPS) * nfw_ref[...]
    h2 = h2 * (1.0 + _mod_rows(sc2_ref, tm)) + _mod_rows(sh2_ref, tm)
    h_scr[...] = h2.astype(BF16)
    d_ff = w2_ref.shape[0]
    for c in range(d_ff // fc):
        gate = jnp.dot(h_scr[...], w1_ref[:, c * fc:(c + 1) * fc], preferred_element_type=F32)
        up = jnp.dot(h_scr[...], w1_ref[:, d_ff + c * fc:d_ff + (c + 1) * fc], preferred_element_type=F32)
        act_scr[:, c * fc:(c + 1) * fc] = (_silu(gate) * up).astype(BF16)
    ffn = jnp.dot(act_scr[...], w2_ref[...], preferred_element_type=F32)
    x2 = x1 + _mod_rows(g2_ref, tm) * ffn
    ms2 = jnp.mean(x2 * x2, axis=-1, keepdims=True)
    y_ref[...] = x2 * lax.rsqrt(ms2 + EPS) * nlw_ref[...]


def _post(x, oa, yc, sgb, mod, lnw, lnb, wo, nfw, w1, w2, nlw, *, tm, fc, rows_per_mod):
    m, d = x.shape
    groups = d // LANES
    mod_spec = lambda row: _mod_spec(mod, row, d, tm, rows_per_mod)
    row_spec = pl.BlockSpec((tm, d), lambda i: (i, 0))
    il_spec = pl.BlockSpec((tm * groups, LANES), lambda i: (i, 0))
    vec = _const_spec((1, d))
    return pl.pallas_call(
        functools.partial(_post_kernel, fc=fc),
        grid=(m // tm,),
        in_specs=[row_spec, row_spec, il_spec, row_spec, mod_spec(2), mod_spec(3), mod_spec(4), mod_spec(5),
                  vec, vec, _const_spec(wo.shape), vec, _const_spec(w1.shape), _const_spec(w2.shape), vec],
        out_specs=row_spec,
        out_shape=jax.ShapeDtypeStruct((m, d), F32),
        scratch_shapes=[pltpu.VMEM((tm, d), F32), pltpu.VMEM((tm, d), BF16),
                        pltpu.VMEM((tm, w2.shape[0]), BF16)],
        compiler_params=_params("parallel"),
        name="post",
    )(x, oa, yc, sgb, mod, mod, mod, mod, lnw, lnb, wo, nfw, w1, w2, nlw)


INPROJ_TC = 256
FFN_CHUNK = 256
MOD_TN = 1024
PROMPT_TM = 512
PROMPT_CHUNK = 128
PROMPT_CHUNKS_PER_STEP = 4
PROMPT_CONV_TILE = 512
PROMPT_CONV_TB = 16
SAMPLE_SEQS = 16
SAMPLE_PAD = 16


def kernel(x_prompt, x_sample, state_hgrn, state_conv, c_prompt, c_sample, w_ada, b_ada, norm_mix_w, w_in, b_in, lb_logits, hgrn_norm_w, conv_dw_w, conv_dw_b, conv_ln_w, conv_ln_b, w_out, norm_ffn_w, w_ffn_in, w_ffn_out, norm_final_w):
    depth = w_in.shape[0]
    assert depth == 1, "single-layer configuration"
    layer = 0
    n_p, l_p, d = x_prompt.shape
    n_s, l_s, _ = x_sample.shape
    heads = d // HEAD_DIM

    w_in_b = w_in[layer].astype(BF16)
    w1 = w_ffn_in[layer].astype(BF16)
    w2 = w_ffn_out[layer].astype(BF16)
    wo = w_out[layer].astype(BF16)
    row = lambda a: a.reshape(1, -1)

    mod = _mod(jnp.concatenate([c_prompt, c_sample], axis=0), w_ada[layer], b_ada[layer], tn=MOD_TN)
    mod_p = mod[:n_p].reshape(n_p, 1, MOD_ROWS * d)
    mod_s = mod[n_p:].reshape(1, n_s, MOD_ROWS * d)

    def inproj(x2, mod_g, tm):
        return _inproj(x2, mod_g, row(norm_mix_w[layer]), w_in_b, row(b_in[layer]), lb_logits,
                       layer=layer, tm=tm, tc=INPROJ_TC, rows_per_mod=tm if mod_g.shape[1] > 1 else l_p)

    def post(x2, oa, yc, sgb, mod_g, tm):
        return _post(x2, oa, yc, sgb, mod_g, row(conv_ln_w[layer]), row(conv_ln_b[layer]), wo,
                     row(norm_ffn_w[layer]), w1, w2, row(norm_final_w),
                     tm=tm, fc=FFN_CHUNK, rows_per_mod=tm if mod_g.shape[1] > 1 else l_p)

    xp = x_prompt.reshape(n_p * l_p, d)
    q, f, v, ga, u, sgb = inproj(xp, mod_p, PROMPT_TM)
    oa, s_p = _recur(q, f, v, ga, hgrn_norm_w[layer], n=n_p, heads=heads, chunk=PROMPT_CHUNK,
                     chunks_per_step=PROMPT_CHUNKS_PER_STEP)
    yc, b_p = _conv(u, conv_dw_w[layer], conv_dw_b[layer], n=n_p, tile=PROMPT_CONV_TILE, tb=PROMPT_CONV_TB)
    y_p = post(xp, oa, yc, sgb, mod_p, PROMPT_TM).reshape(n_p, l_p, d)

    xs = x_sample.transpose(1, 0, 2).reshape(l_s * n_s, d)
    q, f, v, ga, u, sgb = inproj(xs, mod_s, n_s * l_s)
    oa, s_s = _recur_step(q, f, v, ga, hgrn_norm_w[layer], state_hgrn, layer=layer,
                          heads=heads, steps=l_s, seqs=SAMPLE_SEQS, pad=SAMPLE_PAD)
    yc, b_s = _conv_step(u, conv_dw_w[layer], conv_dw_b[layer], state_conv.transpose(0, 2, 1, 3),
                         layer=layer, steps=l_s, seqs=SAMPLE_SEQS)
    y_s = post(xs, oa, yc, sgb, mod_s, n_s * l_s).reshape(l_s, n_s, d).transpose(1, 0, 2)

    return (y_p, y_s, s_p, b_p, s_s, b_s.transpose(0, 2, 1, 3))
```
